```python
import jax, jax.numpy as jnp
from jax import lax
import numpy as np

D_MODEL = 2048
BATCH = 1
SEQ = 8192
DEPTH = 4

GRID_W = 64
CTX_LEN = 256
N_HEADS = 8
D_ATTN = D_MODEL // 2
V_DIM = D_ATTN // N_HEADS
QK_NOPE = 128
QK_ROPE = 64
QK_DIM = QK_NOPE + QK_ROPE
Q_RANK = D_MODEL // 4
KV_RANK = D_MODEL // 8
D_RNN = D_MODEL - D_ATTN
RNN_BLOCKS = 8
RNN_BLOCK = D_RNN // RNN_BLOCKS
CONV_W = 4
LRU_C = 8.0
N_GROUPS = 4
EXPERTS_PER_GROUP = 8
N_EXPERTS = N_GROUPS * EXPERTS_PER_GROUP
EXPERT_TOPK = 2
D_EXPERT = D_MODEL // 4
MOE_BLOCK = 128
Q_BLOCK = 128
ROPE_THETA = 10000.0
EPS = 1e-6
OFF_Q = 0
OFF_KV = OFF_Q + Q_RANK
OFF_KR = OFF_KV + KV_RANK
OFF_RX = OFF_KR + QK_ROPE
OFF_RG = OFF_RX + D_RNN
D_IN = OFF_RG + D_RNN

kernel_name = 'hymba_mla_rglru_hmoe_prefix_dit'


def rms_norm(x, g):
    xf = x.astype(jnp.float32)
    y = xf * lax.rsqrt(jnp.mean(xf * xf, axis=-1, keepdims=True) + EPS)
    return (y * g.astype(jnp.float32)).astype(x.dtype)


def adaln(cond, w_mod, b_mod):
    m = jax.nn.silu(cond) @ w_mod + b_mod
    return jnp.split(m[..., None, :], 6, axis=-1)


def axial_rope(n, dtype):
    rows = n // GRID_W
    row = jnp.repeat(jnp.arange(rows, dtype=jnp.float32), GRID_W)
    col = jnp.tile(jnp.arange(GRID_W, dtype=jnp.float32), rows)
    n_freq = QK_ROPE // 4
    inv = ROPE_THETA ** (-jnp.arange(n_freq, dtype=jnp.float32) / n_freq)
    ang = jnp.concatenate([row[:, None] * inv, col[:, None] * inv], axis=-1)[:, None, :]
    return jnp.cos(ang).astype(dtype), jnp.sin(ang).astype(dtype)


def apply_axial_rope(t, rope):
    if rope is None:
        return t
    cos, sin = rope
    half = QK_ROPE // 2
    t_n = t[..., :QK_NOPE]
    t1 = t[..., QK_NOPE:QK_NOPE + half]
    t2 = t[..., QK_NOPE + half:]
    return jnp.concatenate([t_n, t1 * cos - t2 * sin, t1 * sin + t2 * cos], axis=-1)


def mla_query(p_q, q_a_g, w_uq, q_g, rope):
    B, n, _ = p_q.shape
    q = (rms_norm(p_q, q_a_g) @ w_uq).reshape(B, n, N_HEADS, QK_DIM)
    return apply_axial_rope(rms_norm(q, q_g), rope)


def mla_key_value(p_kv, p_kr, kv_a_g, w_ukv, k_g, rope):
    B, n, _ = p_kv.shape
    kv = (rms_norm(p_kv, kv_a_g) @ w_ukv).reshape(B, n, N_HEADS, QK_NOPE + V_DIM)
    k_nope, v = kv[..., :QK_NOPE], kv[..., QK_NOPE:]
    k_rope = jnp.broadcast_to(p_kr[:, :, None, :], (B, n, N_HEADS, QK_ROPE))
    k = rms_norm(jnp.concatenate([k_nope, k_rope], axis=-1), k_g)
    return apply_axial_rope(k, rope), v


def blocked_attention(q, k, v):
    B, n, H, Dq = q.shape
    nb = n // Q_BLOCK
    qb = jnp.moveaxis(q.reshape(B, nb, Q_BLOCK, H, Dq), 1, 0)
    scale = Dq ** -0.5

    def attend(q_blk):
        s = jnp.einsum('bqhd,bkhd->bhqk', q_blk, k, preferred_element_type=jnp.float32) * scale
        p = jax.nn.softmax(s, axis=-1).astype(v.dtype)
        return jnp.einsum('bhqk,bkhd->bqhd', p, v)

    o = lax.map(attend, qb)
    return jnp.moveaxis(o, 0, 1).reshape(B, n, H * v.shape[-1])


def centred_depthwise_conv(u, w, b):
    left = CONV_W // 2
    y = lax.conv_general_dilated(u, w[:, None, :], window_strides=(1,), padding=[(left, CONV_W - 1 - left)],
                                 dimension_numbers=('NWC', 'WIO', 'NWC'), feature_group_count=u.shape[-1])
    return y + b


def block_diag_linear(u, w, b):
    B, n, C = u.shape
    ub = u.reshape(B, n, RNN_BLOCKS, RNN_BLOCK)
    return jnp.einsum('bnhi,hij->bnhj', ub, w).reshape(B, n, C) + b


def rglru_coeffs(u, wa, ba, wx, bx, lam):
    r = jax.nn.sigmoid(block_diag_linear(u, wa, ba)).astype(jnp.float32)
    i = jax.nn.sigmoid(block_diag_linear(u, wx, bx)).astype(jnp.float32)
    log_a = -LRU_C * r * jax.nn.softplus(-lam.astype(jnp.float32))
    a = jnp.exp(log_a)
    b = jnp.sqrt(-jnp.expm1(2.0 * log_a)) * (i * u.astype(jnp.float32))
    return a, b


def linear_scan(a, b, h0):
    def combine(e1, e2):
        return e1[0] * e2[0], e2[0] * e1[1] + e2[1]
    a_cum, b_cum = lax.associative_scan(combine, (a, b), axis=1)
    return a_cum * h0[:, None, :] + b_cum


def reverse_linear_scan(a, b, h0):
    return jnp.flip(linear_scan(jnp.flip(a, 1), jnp.flip(b, 1), h0), 1)


def merge_groups(attn, rnn, attn_g, rnn_g, w_out):
    y = jnp.concatenate([rms_norm(attn, attn_g), rms_norm(rnn, rnn_g)], axis=-1)
    return y @ w_out


def hybrid_mixer(h_lat, h_ctx, ctx_out, w_in, q_a_g, kv_a_g, w_uq, w_ukv, q_g, k_g, conv_w, conv_b,
                 lru_wa, lru_ba, lru_wx, lru_bx, lru_lambda, attn_g, rnn_g, w_out):
    B, S, _ = h_lat.shape
    base = 0 if ctx_out else OFF_KV
    end = D_IN if ctx_out else OFF_RG
    lat = h_lat @ w_in
    ctx = h_ctx @ w_in[:, base:end]

    def cs(lo, hi):
        return ctx[..., lo - base:hi - base]

    rope = axial_rope(S, h_lat.dtype)
    k_lat, v_lat = mla_key_value(lat[..., OFF_KV:OFF_KR], lat[..., OFF_KR:OFF_RX], kv_a_g, w_ukv, k_g, rope)
    k_ctx, v_ctx = mla_key_value(cs(OFF_KV, OFF_KR), cs(OFF_KR, OFF_RX), kv_a_g, w_ukv, k_g, None)
    q_lat = mla_query(lat[..., OFF_Q:OFF_KV], q_a_g, w_uq, q_g, rope)
    attn_lat = blocked_attention(q_lat, jnp.concatenate([k_ctx, k_lat], axis=1),
                                 jnp.concatenate([v_ctx, v_lat], axis=1))
    u_lat = centred_depthwise_conv(lat[..., OFF_RX:OFF_RG], conv_w, conv_b)
    u_ctx = centred_depthwise_conv(cs(OFF_RX, OFF_RG), conv_w, conv_b)
    h0 = jnp.zeros((B, D_RNN), jnp.float32)
    a, b = rglru_coeffs(u_ctx, lru_wa[0], lru_ba[0], lru_wx[0], lru_bx[0], lru_lambda[0])
    h_cf = linear_scan(a, b, h0)
    a, b = rglru_coeffs(u_ctx, lru_wa[1], lru_ba[1], lru_wx[1], lru_bx[1], lru_lambda[1])
    h_cb = reverse_linear_scan(a, b, h0)
    a, b = rglru_coeffs(u_lat, lru_wa[0], lru_ba[0], lru_wx[0], lru_bx[0], lru_lambda[0])
    h_lf = linear_scan(a, b, h_cf[:, -1])
    a, b = rglru_coeffs(u_lat, lru_wa[1], lru_ba[1], lru_wx[1], lru_bx[1], lru_lambda[1])
    h_lb = reverse_linear_scan(a, b, h_cb[:, 0])
    rnn_lat = jax.nn.gelu(lat[..., OFF_RG:]) * (h_lf + h_lb).astype(h_lat.dtype)
    out_lat = merge_groups(attn_lat, rnn_lat, attn_g, rnn_g, w_out)
    if not ctx_out:
        return out_lat, None
    q_ctx = mla_query(ctx[..., OFF_Q:OFF_KV], q_a_g, w_uq, q_g, None)
    attn_ctx = blocked_attention(q_ctx, k_ctx, v_ctx)
    rnn_ctx = jax.nn.gelu(ctx[..., OFF_RG:]) * (h_cf + h_cb).astype(h_ctx.dtype)
    return out_lat, merge_groups(attn_ctx, rnn_ctx, attn_g, rnn_g, w_out)


def grouped_expert_ffn(t, e_idx, e_w, w_gate_up, w_down):
    T, D = t.shape
    A = T * EXPERT_TOPK
    flat_e = e_idx.reshape(-1)
    flat_tok = jnp.arange(A, dtype=jnp.int32) // EXPERT_TOPK
    flat_w = e_w.reshape(-1)
    order = jnp.argsort(flat_e)
    se, stok, sw = flat_e[order], flat_tok[order], flat_w[order]
    counts = jnp.bincount(flat_e, length=N_EXPERTS)
    padded = (counts + MOE_BLOCK - 1) // MOE_BLOCK * MOE_BLOCK
    pad_end = jnp.cumsum(padded)
    pad_start = pad_end - padded
    start = jnp.cumsum(counts) - counts
    dest = pad_start[se] + jnp.arange(A) - start[se]
    n_blocks = -(-A // MOE_BLOCK) + N_EXPERTS
    n_slots = n_blocks * MOE_BLOCK
    slot_tok = jnp.zeros((n_slots,), jnp.int32).at[dest].set(stok)
    slot_w = jnp.zeros((n_slots,), jnp.float32).at[dest].set(sw)
    blk_e = jnp.minimum(jnp.searchsorted(pad_end, jnp.arange(n_blocks) * MOE_BLOCK, side='right'), N_EXPERTS - 1)
    xs = t[slot_tok].reshape(n_blocks, MOE_BLOCK, D)

    def expert_block(args):
        xb, e = args
        gu = xb @ w_gate_up[e]
        g, u = gu[:, :D_EXPERT], gu[:, D_EXPERT:]
        return (jax.nn.silu(g) * u) @ w_down[e]

    ys = lax.map(expert_block, (xs, blk_e)).reshape(n_slots, D)
    return jnp.zeros((T, D), t.dtype).at[slot_tok].add(ys * slot_w[:, None].astype(ys.dtype))


def hier_moe(h, rg_w, rg_b, re_w, re_b, w_gate_up, w_down):
    B, n, D = h.shape
    t = h.reshape(B * n, D)
    g_prob = jax.nn.softmax((t @ rg_w + rg_b).astype(jnp.float32), axis=-1)
    p_g, g_idx = lax.top_k(g_prob, 1)
    e_logits = (t @ re_w + re_b).astype(jnp.float32).reshape(-1, N_GROUPS, EXPERTS_PER_GROUP)
    e_in = jnp.take_along_axis(e_logits, g_idx[:, :, None], axis=1)[:, 0]
    top_l, top_i = lax.top_k(e_in, EXPERT_TOPK)
    e_w = jax.nn.softmax(top_l, axis=-1) * p_g
    e_idx = g_idx * EXPERTS_PER_GROUP + top_i
    return grouped_expert_ffn(t, e_idx, e_w, w_gate_up, w_down).reshape(B, n, D)


def setup_inputs(seed: int = 0) -> dict:
    key = jax.random.key(seed)
    ks = iter(jax.random.split(key, 40))
    L = DEPTH

    def nrm(shape, scale):
        return jax.random.normal(next(ks), shape, jnp.float32) * scale

    def gain(shape):
        return 1.0 + nrm(shape, 0.02)

    u = jax.random.uniform(next(ks), (L, 2, D_RNN), jnp.float32, 0.9, 0.999)
    s = u ** (1.0 / LRU_C)
    lru_lambda = jnp.log(s) - jnp.log1p(-s)
    return {
        'x': nrm((BATCH, SEQ, D_MODEL), 1.0),
        'c': nrm((BATCH, D_MODEL), 1.0),
        'ctx': nrm((BATCH, CTX_LEN, D_MODEL), 1.0),
        'c_ctx': nrm((D_MODEL,), 1.0),
        'w_mod': nrm((L, D_MODEL, 6 * D_MODEL), 0.5 * D_MODEL ** -0.5),
        'b_mod': nrm((L, 6 * D_MODEL), 0.01),
        'norm1_g': gain((L, D_MODEL)),
        'norm2_g': gain((L, D_MODEL)),
        'w_in': nrm((L, D_MODEL, D_IN), D_MODEL ** -0.5),
        'q_a_norm_g': gain((L, Q_RANK)),
        'kv_a_norm_g': gain((L, KV_RANK)),
        'w_uq': nrm((L, Q_RANK, N_HEADS * QK_DIM), Q_RANK ** -0.5),
        'w_ukv': nrm((L, KV_RANK, N_HEADS * (QK_NOPE + V_DIM)), KV_RANK ** -0.5),
        'q_norm_g': gain((L, QK_DIM)),
        'k_norm_g': gain((L, QK_DIM)),
        'conv_w': nrm((L, CONV_W, D_RNN), CONV_W ** -0.5),
        'conv_b': nrm((L, D_RNN), 0.01),
        'lru_wa': nrm((L, 2, RNN_BLOCKS, RNN_BLOCK, RNN_BLOCK), RNN_BLOCK ** -0.5),
        'lru_ba': nrm((L, 2, D_RNN), 0.01),
        'lru_wx': nrm((L, 2, RNN_BLOCKS, RNN_BLOCK, RNN_BLOCK), RNN_BLOCK ** -0.5),
        'lru_bx': nrm((L, 2, D_RNN), 0.01),
        'lru_lambda': lru_lambda,
        'attn_out_norm_g': gain((L, D_ATTN)),
        'rnn_out_norm_g': gain((L, D_RNN)),
        'w_out': nrm((L, D_ATTN + D_RNN, D_MODEL), (D_ATTN + D_RNN) ** -0.5),
        'router_group_w': nrm((L, D_MODEL, N_GROUPS), D_MODEL ** -0.5),
        'router_group_b': nrm((L, N_GROUPS), 0.01),
        'router_expert_w': nrm((L, D_MODEL, N_EXPERTS), D_MODEL ** -0.5),
        'router_expert_b': nrm((L, N_EXPERTS), 0.01),
        'w_gate_up': nrm((L, N_EXPERTS, D_MODEL, 2 * D_EXPERT), D_MODEL ** -0.5),
        'w_down': nrm((L, N_EXPERTS, D_EXPERT, D_MODEL), D_EXPERT ** -0.5),
    }


def reference(x, c, ctx, c_ctx, w_mod, b_mod, norm1_g, norm2_g, w_in, q_a_norm_g, kv_a_norm_g, w_uq, w_ukv,
              q_norm_g, k_norm_g, conv_w, conv_b, lru_wa, lru_ba, lru_wx, lru_bx, lru_lambda,
              attn_out_norm_g, rnn_out_norm_g, w_out, router_group_w, router_group_b, router_expert_w,
              router_expert_b, w_gate_up, w_down):
    xc = ctx
    C = ctx.shape[1]
    for l in range(DEPTH):
        last = l == DEPTH - 1
        sh1, sc1, g1, sh2, sc2, g2 = adaln(c, w_mod[l], b_mod[l])
        sh1c, sc1c, g1c, sh2c, sc2c, g2c = adaln(c_ctx[None, :], w_mod[l], b_mod[l])
        h_lat = rms_norm(x, norm1_g[l]) * (1 + sc1) + sh1
        h_ctx = rms_norm(xc, norm1_g[l]) * (1 + sc1c) + sh1c
        mix_lat, mix_ctx = hybrid_mixer(
            h_lat, h_ctx, not last, w_in[l], q_a_norm_g[l], kv_a_norm_g[l], w_uq[l], w_ukv[l], q_norm_g[l],
            k_norm_g[l], conv_w[l], conv_b[l], lru_wa[l], lru_ba[l], lru_wx[l], lru_bx[l], lru_lambda[l],
            attn_out_norm_g[l], rnn_out_norm_g[l], w_out[l])
        x = x + g1 * mix_lat
        h2 = rms_norm(x, norm2_g[l]) * (1 + sc2) + sh2
        if last:
            f = hier_moe(h2, router_group_w[l], router_group_b[l], router_expert_w[l], router_expert_b[l],
                         w_gate_up[l], w_down[l])
            x = x + g2 * f
        else:
            xc = xc + g1c * mix_ctx
            h2c = rms_norm(xc, norm2_g[l]) * (1 + sc2c) + sh2c
            f = hier_moe(jnp.concatenate([h2c, h2], axis=1), router_group_w[l], router_group_b[l],
                         router_expert_w[l], router_expert_b[l], w_gate_up[l], w_down[l])
            xc = xc + g2c * f[:, :C]
            x = x + g2 * f[:, C:]
    return x
```

```python
import functools
import math

import jax
import jax.numpy as jnp
from jax import lax
from jax.experimental import pallas as pl
from jax.experimental.pallas import tpu as pltpu

D_MODEL = 2048
SEQ = 8192
CTX_LEN = 256
DEPTH = 4
GRID_W = 64
N_HEADS = 8
D_ATTN = 1024
V_DIM = 128
QK_NOPE = 128
QK_ROPE = 64
QK_DIM = 192
Q_RANK = 512
KV_RANK = 256
D_RNN = 1024
RNN_BLOCKS = 8
RNN_BLOCK = 128
CONV_W = 4
LRU_C = 8.0
N_GROUPS = 4
EXPERTS_PER_GROUP = 8
N_EXPERTS = 32
D_EXPERT = 512
ROPE_THETA = 10000.0
EPS = 1e-6

T_ALL = SEQ + CTX_LEN
TM = 256
NT_ALL = T_ALL // TM
NT_LAT = SEQ // TM
CTX_TILE = NT_LAT
HALO = 8
QK_PAD = 256
V_PAD = 256
KVR_PAD = 384
D_IN_PAD = Q_RANK + KVR_PAD + 2 * D_RNN
ROUTE_LANES = 128
E_OFF = N_GROUPS
MOE_BLK = 256
TQ = 512
TK = 512
VMEM_LIMIT = 56 * 1024 * 1024

_F32 = jnp.float32
_BF16 = jnp.bfloat16


def _cparams(sem):
    return pltpu.CompilerParams(dimension_semantics=sem, vmem_limit_bytes=VMEM_LIMIT)


def _rms(x, g):
    return x * lax.rsqrt(jnp.mean(x * x, axis=-1, keepdims=True) + EPS) * g


def _adaln_kernel(cond_ref, w_ref, b_ref, o_ref):
    cond = cond_ref[...]
    s = cond * jax.nn.sigmoid(cond)
    w = w_ref[...]
    b = b_ref[...]
    o_ref[0:1, :] = jnp.sum(s[:, 0:1] * w, axis=0, keepdims=True) + b
    o_ref[1:2, :] = jnp.sum(s[:, 1:2] * w, axis=0, keepdims=True) + b


def _adaln(cond_t, w_mod, b_mod):
    tn = 1024
    n_col = 6 * D_MODEL // tn
    return pl.pallas_call(
        _adaln_kernel,
        grid=(DEPTH, n_col),
        in_specs=[
            pl.BlockSpec((D_MODEL, 2), lambda l, j: (0, 0)),
            pl.BlockSpec((None, D_MODEL, tn), lambda l, j: (l, 0, j)),
            pl.BlockSpec((None, 1, tn), lambda l, j: (l, 0, j)),
        ],
        out_specs=pl.BlockSpec((None, 2, tn), lambda l, j: (l, 0, j)),
        out_shape=jax.ShapeDtypeStruct((DEPTH, 2, 6 * D_MODEL), _F32),
        compiler_params=_cparams(("parallel", "parallel")),
        name="adaln",
    )(cond_t, w_mod, b_mod.reshape(DEPTH, 1, 6 * D_MODEL))


def _proj_in_kernel(x_ref, mod_ref, g_ref, w_ref, pq_ref, pkv_ref, rx_ref, rg_ref):
    x = x_ref[...]
    h = _rms(x, g_ref[...]) * (1.0 + mod_ref[1:2, :]) + mod_ref[0:1, :]
    hb = h.astype(_BF16)
    o0, o1, o2 = Q_RANK, Q_RANK + KVR_PAD, Q_RANK + KVR_PAD + D_RNN
    pq_ref[...] = jnp.dot(hb, w_ref[:, 0:o0], preferred_element_type=_F32)
    pkv_ref[...] = jnp.dot(hb, w_ref[:, o0:o1], preferred_element_type=_F32)
    rx_ref[...] = jnp.dot(hb, w_ref[:, o1:o2], preferred_element_type=_F32)
    rg_ref[...] = jnp.dot(hb, w_ref[:, o2:D_IN_PAD], preferred_element_type=_F32)


def _proj_in(x, mod_l, g, w):
    row = lambda n: pl.BlockSpec((TM, n), lambda i: (i, 0))
    return pl.pallas_call(
        _proj_in_kernel,
        grid=(NT_ALL,),
        in_specs=[
            row(D_MODEL),
            pl.BlockSpec((None, 6, D_MODEL), lambda i: (i // NT_LAT, 0, 0)),
            pl.BlockSpec((1, D_MODEL), lambda i: (0, 0)),
            pl.BlockSpec((D_MODEL, D_IN_PAD), lambda i: (0, 0)),
        ],
        out_specs=[row(Q_RANK), row(KVR_PAD), row(D_RNN), row(D_RNN)],
        out_shape=[
            jax.ShapeDtypeStruct((T_ALL, Q_RANK), _F32),
            jax.ShapeDtypeStruct((T_ALL, KVR_PAD), _F32),
            jax.ShapeDtypeStruct((T_ALL, D_RNN), _F32),
            jax.ShapeDtypeStruct((T_ALL, D_RNN), _F32),
        ],
        compiler_params=_cparams(("parallel",)),
        name="proj_in",
    )(x, mod_l, g, w)


def _rope(y, c, s1, s2):
    return y * c + pltpu.roll(y, 96, axis=1) * s1 + pltpu.roll(y, 32, axis=1) * s2


def _mla_prep_kernel(pq_ref, pkv_ref, qag_ref, kvag_ref, wuq_ref, wukv_ref, qg_ref, kgn_ref, kgr_ref,
                     c_ref, s1_ref, s2_ref, q_ref, k_ref, v_ref):
    c, s1, s2 = c_ref[...], s1_ref[...], s2_ref[...]
    scale = QK_DIM ** -0.5
    qa = _rms(pq_ref[...], qag_ref[...]).astype(_BF16)
    qf = jnp.dot(qa, wuq_ref[...], preferred_element_type=_F32)
    pkv = pkv_ref[...]
    kva = _rms(pkv[:, 0:KV_RANK], kvag_ref[...]).astype(_BF16)
    kvf = jnp.dot(kva, wukv_ref[...], preferred_element_type=_F32)
    kr = pkv[:, KV_RANK:KVR_PAD]
    kr_ss = jnp.sum(kr * kr, axis=-1, keepdims=True)
    kr_roped = _rope(kr * kgr_ref[...], c, s1, s2)
    qg = qg_ref[...]
    lane = lax.broadcasted_iota(jnp.int32, (TM, V_PAD - V_DIM), 1)
    ones_col = jnp.where(lane == 0, 1.0, 0.0).astype(_BF16)
    for h in range(N_HEADS):
        qh = qf[:, h * QK_PAD:(h + 1) * QK_PAD]
        r = lax.rsqrt(jnp.sum(qh * qh, axis=-1, keepdims=True) * (1.0 / QK_DIM) + EPS)
        qn = qh * r * qg
        q_ref[h, :, 0:QK_NOPE] = (qn[:, 0:QK_NOPE] * scale).astype(_BF16)
        q_ref[h, :, QK_NOPE:QK_PAD] = (_rope(qn[:, QK_NOPE:QK_PAD], c, s1, s2) * scale).astype(_BF16)
        kn = kvf[:, h * 256:h * 256 + QK_NOPE]
        rk = lax.rsqrt((jnp.sum(kn * kn, axis=-1, keepdims=True) + kr_ss) * (1.0 / QK_DIM) + EPS)
        k_ref[h, :, 0:QK_NOPE] = (kn * rk * kgn_ref[...]).astype(_BF16)
        k_ref[h, :, QK_NOPE:QK_PAD] = (kr_roped * rk).astype(_BF16)
        v_ref[h, :, 0:V_DIM] = kvf[:, h * 256 + QK_NOPE:(h + 1) * 256].astype(_BF16)
        v_ref[h, :, V_DIM:V_PAD] = ones_col


def _mla_prep(pq, pkv, qag, kvag, wuq, wukv, qg, kgn, kgr, rope_c, rope_s1, rope_s2):
    row = lambda n: pl.BlockSpec((TM, n), lambda i: (i, 0))
    full = lambda a, b: pl.BlockSpec((a, b), lambda i: (0, 0))
    head = lambda n: pl.BlockSpec((N_HEADS, TM, n), lambda i: (0, i, 0))
    return pl.pallas_call(
        _mla_prep_kernel,
        grid=(NT_ALL,),
        in_specs=[row(Q_RANK), row(KVR_PAD), full(1, Q_RANK), full(1, KV_RANK),
                  full(Q_RANK, N_HEADS * QK_PAD), full(KV_RANK, N_HEADS * 256),
                  full(1, QK_PAD), full(1, QK_NOPE), full(1, 128),
                  row(128), row(128), row(128)],
        out_specs=[head(QK_PAD), head(QK_PAD), head(V_PAD)],
        out_shape=[jax.ShapeDtypeStruct((N_HEADS, T_ALL, QK_PAD), _BF16),
                   jax.ShapeDtypeStruct((N_HEADS, T_ALL, QK_PAD), _BF16),
                   jax.ShapeDtypeStruct((N_HEADS, T_ALL, V_PAD), _BF16)],
        compiler_params=_cparams(("parallel",)),
        name="mla_prep",
    )(pq, pkv, qag, kvag, wuq, wukv, qg, kgn, kgr, rope_c, rope_s1, rope_s2)


def _attn_kernel(q_ref, k_ref, v_ref, o_ref, *, n_keys, tk):
    q = q_ref[...]
    tq = q.shape[0]

    def step(j, carry):
        m, acc = carry
        start = pl.multiple_of(j * tk, tk)
        kc = k_ref[pl.ds(start, tk), :]
        vc = v_ref[pl.ds(start, tk), :]
        s = lax.dot_general(q, kc, (((1,), (1,)), ((), ())), preferred_element_type=_F32)
        m_new = jnp.maximum(m, jnp.max(s, axis=-1, keepdims=True))
        alpha = jnp.exp(m - m_new)
        p = jnp.exp(s - m_new)
        acc = alpha * acc + jnp.dot(p.astype(_BF16), vc, preferred_element_type=_F32)
        return m_new, acc

    m0 = jnp.full((tq, 1), -jnp.inf, _F32)
    acc0 = jnp.zeros((tq, V_PAD), _F32)
    _, acc = lax.fori_loop(0, n_keys // tk, step, (m0, acc0))
    o_ref[...] = acc[:, 0:V_DIM] / acc[:, V_DIM:V_DIM + 1]


def _attention(q, k, v, *, q_tile0, n_q, tq, key_tile0, n_keys, tk):
    return pl.pallas_call(
        functools.partial(_attn_kernel, n_keys=n_keys, tk=tk),
        grid=(N_HEADS, n_q // tq),
        in_specs=[
            pl.BlockSpec((None, tq, QK_PAD), lambda h, i: (h, q_tile0 + i, 0)),
            pl.BlockSpec((None, n_keys, QK_PAD), lambda h, i: (h, key_tile0, 0)),
            pl.BlockSpec((None, n_keys, V_PAD), lambda h, i: (h, key_tile0, 0)),
        ],
        out_specs=pl.BlockSpec((tq, V_DIM), lambda h, i: (i, h)),
        out_shape=jax.ShapeDtypeStruct((n_q, D_ATTN), _F32),
        compiler_params=_cparams(("parallel", "parallel")),
        name="attention",
    )(q, k, v)


def _gelu_tanh(x):
    return 0.5 * x * (1.0 + jnp.tanh(math.sqrt(2.0 / math.pi) * (x + 0.044715 * (x * x * x))))


def _scan_tile(a, b, carry, reverse):
    rows = lax.broadcasted_iota(jnp.int32, (TM, 1), 0) % 8
    for s in (1, 2, 4):
        if reverse:
            a_sh, b_sh = pltpu.roll(a, TM - s, axis=0), pltpu.roll(b, TM - s, axis=0)
            ok = rows < 8 - s
        else:
            a_sh, b_sh = pltpu.roll(a, s, axis=0), pltpu.roll(b, s, axis=0)
            ok = rows >= s
        b = jnp.where(ok, a * b_sh + b, b)
        a = jnp.where(ok, a * a_sh, a)
    out = [None] * (TM // 8)
    order = range(TM // 8 - 1, -1, -1) if reverse else range(TM // 8)
    for g in order:
        h = a[g * 8:(g + 1) * 8, :] * carry + b[g * 8:(g + 1) * 8, :]
        carry = h[0:1, :] if reverse else h[7:8, :]
        out[g] = h
    return jnp.concatenate(out, axis=0), carry


def _rnn_kernel(rx_ref, rxp_ref, rxn_ref, cw_ref, cb_ref, wa_ref, ba_ref, wx_ref, bx_ref, lam_ref,
                *rest, reverse, final):
    if final:
        hf_ref, rg_ref, g_ref, o_ref, carry_ref = rest
    else:
        o_ref, carry_ref = rest
    j = pl.program_id(0)
    tile = jnp.where(j == 0, CTX_TILE, (NT_LAT - j) if reverse else (j - 1))

    @pl.when(j == 0)
    def _():
        carry_ref[...] = jnp.zeros_like(carry_ref)

    has_prev = jnp.logical_and(tile >= 1, tile < NT_LAT).astype(_F32)
    has_next = (tile < NT_LAT - 1).astype(_F32)
    cur = rx_ref[...]
    p6 = rxp_ref[HALO - 2:HALO - 1, :] * has_prev
    p7 = rxp_ref[HALO - 1:HALO, :] * has_prev
    n0 = rxn_ref[0:1, :] * has_next
    rows = lax.broadcasted_iota(jnp.int32, (TM, 1), 0)
    um2 = jnp.where(rows == 0, p6, jnp.where(rows == 1, p7, pltpu.roll(cur, 2, axis=0)))
    um1 = jnp.where(rows == 0, p7, pltpu.roll(cur, 1, axis=0))
    up1 = jnp.where(rows == TM - 1, n0, pltpu.roll(cur, TM - 1, axis=0))
    u = (cw_ref[0:1, :] * um2 + cw_ref[1:2, :] * um1 + cw_ref[2:3, :] * cur + cw_ref[3:4, :] * up1
         + cb_ref[...])

    ub = u.astype(_BF16)
    r_parts, i_parts = [], []
    for h in range(RNN_BLOCKS):
        uh = ub[:, h * RNN_BLOCK:(h + 1) * RNN_BLOCK]
        r_parts.append(jnp.dot(uh, wa_ref[h], preferred_element_type=_F32))
        i_parts.append(jnp.dot(uh, wx_ref[h], preferred_element_type=_F32))
    r = jax.nn.sigmoid(jnp.concatenate(r_parts, axis=-1) + ba_ref[...])
    i = jax.nn.sigmoid(jnp.concatenate(i_parts, axis=-1) + bx_ref[...])
    nl = -lam_ref[...]
    softplus = jnp.maximum(nl, 0.0) + jnp.log1p(jnp.exp(-jnp.abs(nl)))
    a = jnp.exp((-LRU_C) * r * softplus)
    b = jnp.sqrt(1.0 - a * a) * (i * u)

    h, carry = _scan_tile(a, b, carry_ref[0:1, :], reverse)
    carry_ref[0:1, :] = carry
    if final:
        rnn = _gelu_tanh(rg_ref[...]) * (hf_ref[...] + h)
        o_ref[...] = _rms(rnn, g_ref[...])
    else:
        o_ref[...] = h


def _rnn(l, d, rx, conv_w, conv_b, wa, ba, wx, bx, lam, extra=None):
    reverse = d == 1
    final = extra is not None

    def tile_of(j):
        return jnp.where(j == 0, CTX_TILE, (NT_LAT - j) if reverse else (j - 1))

    n8 = TM // HALO
    row = pl.BlockSpec((TM, D_RNN), lambda j: (tile_of(j), 0))
    vec = lambda a: pl.BlockSpec((None, None, 1, D_RNN), lambda j: (l, d, 0, 0))
    mat = pl.BlockSpec((None, None, RNN_BLOCKS, RNN_BLOCK, RNN_BLOCK), lambda j: (l, d, 0, 0, 0))
    in_specs = [
        row,
        pl.BlockSpec((HALO, D_RNN), lambda j: (jnp.maximum(tile_of(j) * n8 - 1, 0), 0)),
        pl.BlockSpec((HALO, D_RNN), lambda j: (jnp.minimum((tile_of(j) + 1) * n8, T_ALL // HALO - 1), 0)),
        pl.BlockSpec((None, CONV_W, D_RNN), lambda j: (l, 0, 0)),
        pl.BlockSpec((None, 1, D_RNN), lambda j: (l, 0, 0)),
        mat, vec(ba), mat, vec(bx), vec(lam),
    ]
    args = [rx, rx, rx, conv_w, conv_b.reshape(DEPTH, 1, D_RNN), wa, ba.reshape(DEPTH, 2, 1, D_RNN),
            wx, bx.reshape(DEPTH, 2, 1, D_RNN), lam.reshape(DEPTH, 2, 1, D_RNN)]
    if final:
        hf, rg, g = extra
        in_specs += [row, row, pl.BlockSpec((1, D_RNN), lambda j: (0, 0))]
        args += [hf, rg, g]
    return pl.pallas_call(
        functools.partial(_rnn_kernel, reverse=reverse, final=final),
        grid=(NT_ALL,),
        in_specs=in_specs,
        out_specs=row,
        out_shape=jax.ShapeDtypeStruct((T_ALL, D_RNN), _F32),
        scratch_shapes=[pltpu.VMEM((8, D_RNN), _F32)],
        compiler_params=_cparams(("arbitrary",)),
        name="rnn_bwd" if reverse else "rnn_fwd",
    )(*args)


def _out_proj_kernel(alat_ref, actx_ref, yr_ref, ag_ref, w_ref, x_ref, mod_ref, g2_ref, wr_ref, br_ref,
                     xo_ref, h2_ref, ri_ref, rw_ref, cnt_ref, carry_ref):
    i = pl.program_id(0)

    @pl.when(i == 0)
    def _():
        carry_ref[...] = jnp.zeros_like(carry_ref)

    a = jnp.where(i == CTX_TILE, actx_ref[...], alat_ref[...])
    ya = _rms(a, ag_ref[...]).astype(_BF16)
    yr = yr_ref[...].astype(_BF16)
    mix = (jnp.dot(ya, w_ref[0:D_ATTN, :], preferred_element_type=_F32)
           + jnp.dot(yr, w_ref[D_ATTN:D_MODEL, :], preferred_element_type=_F32))
    x = x_ref[...] + mod_ref[2:3, :] * mix
    xo_ref[...] = x
    h2 = _rms(x, g2_ref[...]) * (1.0 + mod_ref[4:5, :]) + mod_ref[3:4, :]
    h2_ref[...] = h2

    logits = jnp.dot(h2, wr_ref[...], preferred_element_type=_F32, precision=lax.Precision.HIGHEST)
    logits = logits + br_ref[...]
    lane = lax.broadcasted_iota(jnp.int32, (TM, ROUTE_LANES), 1)
    lane_f = lane.astype(_F32)
    big = jnp.float32(1 << 20)
    neg = -jnp.inf
    gl = jnp.where(lane < N_GROUPS, logits, neg)
    gmax = jnp.max(gl, axis=-1, keepdims=True)
    gsum = jnp.sum(jnp.exp(gl - gmax), axis=-1, keepdims=True)
    p_g = 1.0 / gsum
    g_idx = jnp.min(jnp.where(gl == gmax, lane_f, big), axis=-1, keepdims=True).astype(jnp.int32)
    lo = E_OFF + EXPERTS_PER_GROUP * g_idx
    el = jnp.where(jnp.logical_and(lane >= lo, lane < lo + EXPERTS_PER_GROUP), logits, neg)
    m1 = jnp.max(el, axis=-1, keepdims=True)
    i1 = jnp.min(jnp.where(el == m1, lane_f, big), axis=-1, keepdims=True).astype(jnp.int32)
    el2 = jnp.where(lane == i1, neg, el)
    m2 = jnp.max(el2, axis=-1, keepdims=True)
    i2 = jnp.min(jnp.where(el2 == m2, lane_f, big), axis=-1, keepdims=True).astype(jnp.int32)
    e21 = jnp.exp(m2 - m1)
    w1 = p_g / (1.0 + e21)
    w2 = p_g * e21 / (1.0 + e21)

    hot1 = lane == i1
    hot2 = lane == i2
    hot = jnp.logical_or(hot1, hot2).astype(_F32)
    rr = lax.broadcasted_iota(jnp.int32, (TM, TM), 0)
    cc = lax.broadcasted_iota(jnp.int32, (TM, TM), 1)
    tri = (cc < rr).astype(_BF16)
    before = jnp.dot(tri, hot.astype(_BF16), preferred_element_type=_F32) + carry_ref[0:1, :]
    rank1 = jnp.sum(jnp.where(hot1, before, 0.0), axis=-1, keepdims=True).astype(jnp.int32)
    rank2 = jnp.sum(jnp.where(hot2, before, 0.0), axis=-1, keepdims=True).astype(jnp.int32)
    carry = carry_ref[0:1, :] + jnp.sum(hot, axis=0, keepdims=True)
    carry_ref[0:1, :] = carry
    cnt_ref[...] = jnp.broadcast_to(carry, cnt_ref.shape).astype(jnp.int32)

    zi = jnp.zeros((TM, ROUTE_LANES), jnp.int32)
    ri_ref[...] = jnp.where(lane == 0, i1 - E_OFF, jnp.where(lane == 1, i2 - E_OFF,
                            jnp.where(lane == 2, rank1, jnp.where(lane == 3, rank2, zi))))
    rw_ref[...] = jnp.where(lane == 0, w1, jnp.where(lane == 1, w2, 0.0))


def _out_proj(n_tiles, alat, actx, yr, ag, w, x, mod_l, g2, wr, br):
    row = lambda n: pl.BlockSpec((TM, n), lambda i: (i, 0))
    full = lambda a, b: pl.BlockSpec((a, b), lambda i: (0, 0))
    n_rows = n_tiles * TM
    return pl.pallas_call(
        _out_proj_kernel,
        grid=(n_tiles,),
        in_specs=[
            pl.BlockSpec((TM, D_ATTN), lambda i: (jnp.minimum(i, NT_LAT - 1), 0)),
            full(TM, D_ATTN),
            row(D_RNN), full(1, D_ATTN), full(D_MODEL, D_MODEL), row(D_MODEL),
            pl.BlockSpec((None, 6, D_MODEL), lambda i: (i // NT_LAT, 0, 0)),
            full(1, D_MODEL), full(D_MODEL, ROUTE_LANES), full(1, ROUTE_LANES),
        ],
        out_specs=[row(D_MODEL), row(D_MODEL), row(ROUTE_LANES), row(ROUTE_LANES), full(8, ROUTE_LANES)],
        out_shape=[
            jax.ShapeDtypeStruct((n_rows, D_MODEL), _F32),
            jax.ShapeDtypeStruct((n_rows, D_MODEL), _F32),
            jax.ShapeDtypeStruct((n_rows, ROUTE_LANES), jnp.int32),
            jax.ShapeDtypeStruct((n_rows, ROUTE_LANES), _F32),
            jax.ShapeDtypeStruct((8, ROUTE_LANES), jnp.int32),
        ],
        scratch_shapes=[pltpu.VMEM((8, ROUTE_LANES), _F32)],
        compiler_params=_cparams(("arbitrary",)),
        name="out_proj",
    )(alat, actx, yr, ag, w, x, mod_l, g2, wr, br)


def _dispatch_kernel(dest_ref, h2_hbm, xs_in_hbm, xs_hbm, sem):
    del xs_in_hbm
    i = pl.program_id(0)

    def copy(t, k):
        row = i * TM + t
        return pltpu.make_async_copy(h2_hbm.at[pl.ds(row, 1), :],
                                     xs_hbm.at[pl.ds(dest_ref[2 * row + k], 1), :], sem)

    def issue(t, c):
        copy(t, 0).start()
        copy(t, 1).start()
        return c

    def drain(t, c):
        copy(t, 0).wait()
        copy(t, 1).wait()
        return c

    lax.fori_loop(0, TM, issue, 0)
    lax.fori_loop(0, TM, drain, 0)


def _dispatch(n_tiles, n_slots, dest_flat, h2):
    xs0 = jnp.zeros((n_slots, D_MODEL), _F32)
    return pl.pallas_call(
        _dispatch_kernel,
        grid_spec=pltpu.PrefetchScalarGridSpec(
            num_scalar_prefetch=1,
            grid=(n_tiles,),
            in_specs=[pl.BlockSpec(memory_space=pl.ANY), pl.BlockSpec(memory_space=pl.ANY)],
            out_specs=pl.BlockSpec(memory_space=pl.ANY),
            scratch_shapes=[pltpu.SemaphoreType.DMA(())],
        ),
        out_shape=jax.ShapeDtypeStruct((n_slots, D_MODEL), _F32),
        input_output_aliases={2: 0},
        compiler_params=_cparams(("arbitrary",)),
        name="moe_dispatch",
    )(dest_flat, h2, xs0)


def _expert_kernel(blk_e_ref, x_ref, wgu_ref, wd_ref, y_ref):
    del blk_e_ref
    gu = jnp.dot(x_ref[...], wgu_ref[...], preferred_element_type=_F32)
    g, u = gu[:, 0:D_EXPERT], gu[:, D_EXPERT:2 * D_EXPERT]
    act = g * jax.nn.sigmoid(g) * u
    y_ref[...] = jnp.dot(act, wd_ref[...], preferred_element_type=_F32)


def _experts(l, n_blocks, blk_e, xs, w_gate_up, w_down):
    return pl.pallas_call(
        _expert_kernel,
        grid_spec=pltpu.PrefetchScalarGridSpec(
            num_scalar_prefetch=1,
            grid=(n_blocks,),
            in_specs=[
                pl.BlockSpec((MOE_BLK, D_MODEL), lambda b, be: (b, 0)),
                pl.BlockSpec((None, None, D_MODEL, 2 * D_EXPERT), lambda b, be: (l, be[b], 0, 0)),
                pl.BlockSpec((None, None, D_EXPERT, D_MODEL), lambda b, be: (l, be[b], 0, 0)),
            ],
            out_specs=pl.BlockSpec((MOE_BLK, D_MODEL), lambda b, be: (b, 0)),
        ),
        out_shape=jax.ShapeDtypeStruct((n_blocks * MOE_BLK, D_MODEL), _F32),
        compiler_params=_cparams(("arbitrary",)),
        name="moe_experts",
    )(blk_e, xs, w_gate_up, w_down)


def _combine_kernel(dest_ref, ys_hbm, x_ref, rw_ref, mod_ref, o_ref, buf, sem):
    i = pl.program_id(0)

    def copy(t, k):
        return pltpu.make_async_copy(ys_hbm.at[pl.ds(dest_ref[2 * (i * TM + t) + k], 1), :],
                                     buf.at[k, pl.ds(t, 1), :], sem)

    def issue(t, c):
        copy(t, 0).start()
        copy(t, 1).start()
        return c

    def drain(t, c):
        copy(t, 0).wait()
        copy(t, 1).wait()
        return c

    lax.fori_loop(0, TM, issue, 0)
    lax.fori_loop(0, TM, drain, 0)
    rw = rw_ref[...]
    f = buf[0] * rw[:, 0:1] + buf[1] * rw[:, 1:2]
    o_ref[...] = x_ref[...] + mod_ref[5:6, :] * f


def _combine(n_tiles, dest_flat, ys, x, rw, mod_l):
    return pl.pallas_call(
        _combine_kernel,
        grid_spec=pltpu.PrefetchScalarGridSpec(
            num_scalar_prefetch=1,
            grid=(n_tiles,),
            in_specs=[
                pl.BlockSpec(memory_space=pl.ANY),
                pl.BlockSpec((TM, D_MODEL), lambda i, d: (i, 0)),
                pl.BlockSpec((TM, ROUTE_LANES), lambda i, d: (i, 0)),
                pl.BlockSpec((None, 6, D_MODEL), lambda i, d: (i // NT_LAT, 0, 0)),
            ],
            out_specs=pl.BlockSpec((TM, D_MODEL), lambda i, d: (i, 0)),
            scratch_shapes=[pltpu.VMEM((2, TM, D_MODEL), _F32), pltpu.SemaphoreType.DMA(())],
        ),
        out_shape=jax.ShapeDtypeStruct((n_tiles * TM, D_MODEL), _F32),
        compiler_params=_cparams(("arbitrary",)),
        name="moe_combine",
    )(dest_flat, ys, x, rw, mod_l)


def _moe(l, n_tiles, h2, ri, rw, counts, x, mod_l, w_gate_up, w_down):
    n_tok = n_tiles * TM
    n_blocks = n_tok * 2 // MOE_BLK + N_EXPERTS
    cnt = counts[0, E_OFF:E_OFF + N_EXPERTS]
    padded = (cnt + MOE_BLK - 1) // MOE_BLK * MOE_BLK
    pad_end = jnp.cumsum(padded)
    pad_start = pad_end - padded
    dest = (pad_start[ri[:, 0:2]] + ri[:, 2:4]).reshape(-1).astype(jnp.int32)
    blk_e = jnp.minimum(jnp.searchsorted(pad_end, jnp.arange(n_blocks, dtype=jnp.int32) * MOE_BLK, side='right'),
                        N_EXPERTS - 1).astype(jnp.int32)
    xs = _dispatch(n_tiles, n_blocks * MOE_BLK, dest, h2)
    ys = _experts(l, n_blocks, blk_e, xs, w_gate_up, w_down)
    return _combine(n_tiles, dest, ys, x, rw, mod_l)


def _rope_tables():
    pos = jnp.arange(SEQ, dtype=_F32)
    row = jnp.floor(pos / GRID_W)
    col = pos - row * GRID_W
    n_freq = QK_ROPE // 4
    inv = ROPE_THETA ** (-jnp.arange(n_freq, dtype=_F32) / n_freq)
    ang = jnp.concatenate([row[:, None] * inv, col[:, None] * inv], axis=-1)
    cos, sin = jnp.cos(ang), jnp.sin(ang)
    z32 = jnp.zeros((SEQ, 32), _F32)
    z64 = jnp.zeros((SEQ, 64), _F32)
    c = jnp.concatenate([cos, cos, jnp.ones((SEQ, 64), _F32)], axis=-1)
    s1 = jnp.concatenate([-sin, z32, z64], axis=-1)
    s2 = jnp.concatenate([z32, sin, z64], axis=-1)
    ctx_c = jnp.ones((CTX_LEN, 128), _F32)
    ctx_s = jnp.zeros((CTX_LEN, 128), _F32)
    return (jnp.concatenate([c, ctx_c]), jnp.concatenate([s1, ctx_s]), jnp.concatenate([s2, ctx_s]))


def _pad_head_gain(g):
    return jnp.pad(g, ((0, 0), (0, QK_PAD - QK_DIM)))[:, None, :]


def kernel(x, c, ctx, c_ctx, w_mod, b_mod, norm1_g, norm2_g, w_in, q_a_norm_g, kv_a_norm_g, w_uq, w_ukv, q_norm_g, k_norm_g, conv_w, conv_b, lru_wa, lru_ba, lru_wx, lru_bx, lru_lambda, attn_out_norm_g, rnn_out_norm_g, w_out, router_group_w, router_group_b, router_expert_w, router_expert_b, w_gate_up, w_down):
    L = DEPTH
    zc = jnp.zeros((L, D_MODEL, 64), _F32)
    w_in_p = jnp.concatenate([w_in[:, :, 0:832], zc, w_in[:, :, 832:2880]], axis=-1).astype(_BF16)
    w_uq_p = jnp.pad(w_uq.reshape(L, Q_RANK, N_HEADS, QK_DIM), ((0, 0), (0, 0), (0, 0), (0, QK_PAD - QK_DIM)))
    w_uq_p = w_uq_p.reshape(L, Q_RANK, N_HEADS * QK_PAD).astype(_BF16)
    w_ukv_b = w_ukv.astype(_BF16)
    w_out_b = w_out.astype(_BF16)
    qg_p = _pad_head_gain(q_norm_g)
    kgn = k_norm_g[:, None, 0:QK_NOPE]
    kgr = jnp.pad(k_norm_g[:, QK_NOPE:QK_DIM], ((0, 0), (0, 64)))[:, None, :]
    wr = jnp.pad(jnp.concatenate([router_group_w, router_expert_w], axis=-1),
                 ((0, 0), (0, 0), (0, ROUTE_LANES - N_GROUPS - N_EXPERTS)))
    br = jnp.pad(jnp.concatenate([router_group_b, router_expert_b], axis=-1),
                 ((0, 0), (0, ROUTE_LANES - N_GROUPS - N_EXPERTS)))[:, None, :]
    rope_c, rope_s1, rope_s2 = _rope_tables()

    cond_t = jnp.stack([c[0], c_ctx], axis=-1)
    mod = _adaln(cond_t, w_mod, b_mod).reshape(L, 2, 6, D_MODEL)

    xs = jnp.concatenate([x[0], ctx[0]], axis=0)
    for l in range(L):
        last = l == L - 1
        n_tiles = NT_LAT if last else NT_ALL
        pq, pkv, rx, rg = _proj_in(xs, mod[l], norm1_g[l][None, :], w_in_p[l])
        q, k, v = _mla_prep(pq, pkv, q_a_norm_g[l][None, :], kv_a_norm_g[l][None, :], w_uq_p[l], w_ukv_b[l],
                            qg_p[l], kgn[l], kgr[l], rope_c, rope_s1, rope_s2)
        attn_lat = _attention(q, k, v, q_tile0=0, n_q=SEQ, tq=TQ, key_tile0=0, n_keys=T_ALL, tk=TM * 3)
        if last:
            attn_ctx = attn_lat[0:TM]
        else:
            attn_ctx = _attention(q, k, v, q_tile0=CTX_TILE, n_q=CTX_LEN, tq=CTX_LEN, key_tile0=CTX_TILE,
                                  n_keys=CTX_LEN, tk=CTX_LEN)
        h_f = _rnn(l, 0, rx, conv_w, conv_b, lru_wa, lru_ba, lru_wx, lru_bx, lru_lambda)
        y_rnn = _rnn(l, 1, rx, conv_w, conv_b, lru_wa, lru_ba, lru_wx, lru_bx, lru_lambda,
                     extra=(h_f, rg, rnn_out_norm_g[l][None, :]))
        x_mid, h2, ri, rw, counts = _out_proj(n_tiles, attn_lat, attn_ctx, y_rnn, attn_out_norm_g[l][None, :],
                                              w_out_b[l], xs, mod[l], norm2_g[l][None, :], wr[l], br[l])
        xs = _moe(l, n_tiles, h2, ri, rw, counts, x_mid, mod[l], w_gate_up, w_down)
    return xs.reshape(1, SEQ, D_MODEL)
```

```python
import functools
import math

import jax
import jax.numpy as jnp
from jax import lax
from jax.experimental import pallas as pl
from jax.experimental.pallas import tpu as pltpu

D_MODEL = 2048
SEQ = 8192
CTX_LEN = 256
DEPTH = 4
GRID_W = 64
N_HEADS = 8
D_ATTN = 1024
V_DIM = 128
QK_NOPE = 128
QK_ROPE = 64
QK_DIM = 192
Q_RANK = 512
KV_RANK = 256
D_RNN = 1024
RNN_BLOCKS = 8
RNN_BLOCK = 128
CONV_W = 4
LRU_C = 8.0
N_GROUPS = 4
EXPERTS_PER_GROUP = 8
N_EXPERTS = 32
D_EXPERT = 512
ROPE_THETA = 10000.0
EPS = 1e-6

T_ALL = SEQ + CTX_LEN
TM = 256
NT_ALL = T_ALL // TM
NT_LAT = SEQ // TM
CTX_TILE = NT_LAT
HALO = 8
QK_PAD = 256
V_PAD = 256
KVR_PAD = 384
D_IN_PAD = Q_RANK + KVR_PAD + 2 * D_RNN
ROUTE_LANES = 128
E_OFF = N_GROUPS
MOE_BLK = 256
TQ = 512
TK = 512
VMEM_LIMIT = 56 * 1024 * 1024

_F32 = jnp.float32
_BF16 = jnp.bfloat16


def _cparams(sem):
    return pltpu.CompilerParams(dimension_semantics=sem, vmem_limit_bytes=VMEM_LIMIT)


def _rms(x, g):
    return x * lax.rsqrt(jnp.mean(x * x, axis=-1, keepdims=True) + EPS) * g


def _adaln_kernel(cond_ref, w_ref, b_ref, o_ref):
    cond = cond_ref[...]
    s = cond * jax.nn.sigmoid(cond)
    w = w_ref[...]
    b = b_ref[...]
    o_ref[0:1, :] = jnp.sum(s[:, 0:1] * w, axis=0, keepdims=True) + b
    o_ref[1:2, :] = jnp.sum(s[:, 1:2] * w, axis=0, keepdims=True) + b


def _adaln(cond_t, w_mod, b_mod):
    tn = 1024
    n_col = 6 * D_MODEL // tn
    return pl.pallas_call(
        _adaln_kernel,
        grid=(DEPTH, n_col),
        in_specs=[
            pl.BlockSpec((D_MODEL, 2), lambda l, j: (0, 0)),
            pl.BlockSpec((None, D_MODEL, tn), lambda l, j: (l, 0, j)),
            pl.BlockSpec((None, 1, tn), lambda l, j: (l, 0, j)),
        ],
        out_specs=pl.BlockSpec((None, 2, tn), lambda l, j: (l, 0, j)),
        out_shape=jax.ShapeDtypeStruct((DEPTH, 2, 6 * D_MODEL), _F32),
        compiler_params=_cparams(("parallel", "parallel")),
        name="adaln",
    )(cond_t, w_mod, b_mod.reshape(DEPTH, 1, 6 * D_MODEL))


def _proj_in_kernel(x_ref, mod_ref, g_ref, w_ref, pq_ref, pkv_ref, rx_ref, rg_ref):
    x = x_ref[...]
    h = _rms(x, g_ref[...]) * (1.0 + mod_ref[1:2, :]) + mod_ref[0:1, :]
    hb = h.astype(_BF16)
    o0, o1, o2 = Q_RANK, Q_RANK + KVR_PAD, Q_RANK + KVR_PAD + D_RNN
    pq_ref[...] = jnp.dot(hb, w_ref[:, 0:o0], preferred_element_type=_F32)
    pkv_ref[...] = jnp.dot(hb, w_ref[:, o0:o1], preferred_element_type=_F32)
    rx_ref[...] = jnp.dot(hb, w_ref[:, o1:o2], preferred_element_type=_F32)
    rg_ref[...] = jnp.dot(hb, w_ref[:, o2:D_IN_PAD], preferred_element_type=_F32)


def _proj_in(x, mod_l, g, w):
    row = lambda n: pl.BlockSpec((TM, n), lambda i: (i, 0))
    return pl.pallas_call(
        _proj_in_kernel,
        grid=(NT_ALL,),
        in_specs=[
            row(D_MODEL),
            pl.BlockSpec((None, 6, D_MODEL), lambda i: (i // NT_LAT, 0, 0)),
            pl.BlockSpec((1, D_MODEL), lambda i: (0, 0)),
            pl.BlockSpec((D_MODEL, D_IN_PAD), lambda i: (0, 0)),
        ],
        out_specs=[row(Q_RANK), row(KVR_PAD), row(D_RNN), row(D_RNN)],
        out_shape=[
            jax.ShapeDtypeStruct((T_ALL, Q_RANK), _F32),
            jax.ShapeDtypeStruct((T_ALL, KVR_PAD), _F32),
            jax.ShapeDtypeStruct((T_ALL, D_RNN), _F32),
            jax.ShapeDtypeStruct((T_ALL, D_RNN), _F32),
        ],
        compiler_params=_cparams(("parallel",)),
        name="proj_in",
    )(x, mod_l, g, w)


def _rope(y, c, s1, s2):
    return y * c + pltpu.roll(y, 96, axis=1) * s1 + pltpu.roll(y, 32, axis=1) * s2


def _mla_prep_kernel(pq_ref, pkv_ref, qag_ref, kvag_ref, wuq_ref, wukv_ref, qg_ref, kgn_ref, kgr_ref,
                     c_ref, s1_ref, s2_ref, q_ref, k_ref, v_ref):
    c, s1, s2 = c_ref[...], s1_ref[...], s2_ref[...]
    scale = QK_DIM ** -0.5 * math.log2(math.e)
    qa = _rms(pq_ref[...], qag_ref[...]).astype(_BF16)
    qf = jnp.dot(qa, wuq_ref[...], preferred_element_type=_F32)
    pkv = pkv_ref[...]
    kva = _rms(pkv[:, 0:KV_RANK], kvag_ref[...]).astype(_BF16)
    kvf = jnp.dot(kva, wukv_ref[...], preferred_element_type=_F32)
    kr = pkv[:, KV_RANK:KVR_PAD]
    kr_ss = jnp.sum(kr * kr, axis=-1, keepdims=True)
    kr_roped = _rope(kr * kgr_ref[...], c, s1, s2)
    qg = qg_ref[...]
    lane = lax.broadcasted_iota(jnp.int32, (TM, V_PAD - V_DIM), 1)
    ones_col = jnp.where(lane == 0, 1.0, 0.0).astype(_BF16)
    for h in range(N_HEADS):
        qh = qf[:, h * QK_PAD:(h + 1) * QK_PAD]
        r = lax.rsqrt(jnp.sum(qh * qh, axis=-1, keepdims=True) * (1.0 / QK_DIM) + EPS)
        qn = qh * r * qg
        q_ref[h, :, 0:QK_NOPE] = (qn[:, 0:QK_NOPE] * scale).astype(_BF16)
        q_ref[h, :, QK_NOPE:QK_PAD] = (_rope(qn[:, QK_NOPE:QK_PAD], c, s1, s2) * scale).astype(_BF16)
        kn = kvf[:, h * 256:h * 256 + QK_NOPE]
        rk = lax.rsqrt((jnp.sum(kn * kn, axis=-1, keepdims=True) + kr_ss) * (1.0 / QK_DIM) + EPS)
        k_ref[h, :, 0:QK_NOPE] = (kn * rk * kgn_ref[...]).astype(_BF16)
        k_ref[h, :, QK_NOPE:QK_PAD] = (kr_roped * rk).astype(_BF16)
        v_ref[h, :, 0:V_DIM] = kvf[:, h * 256 + QK_NOPE:(h + 1) * 256].astype(_BF16)
        v_ref[h, :, V_DIM:V_PAD] = ones_col


def _mla_prep(pq, pkv, qag, kvag, wuq, wukv, qg, kgn, kgr, rope_c, rope_s1, rope_s2):
    row = lambda n: pl.BlockSpec((TM, n), lambda i: (i, 0))
    full = lambda a, b: pl.BlockSpec((a, b), lambda i: (0, 0))
    head = lambda n: pl.BlockSpec((N_HEADS, TM, n), lambda i: (0, i, 0))
    return pl.pallas_call(
        _mla_prep_kernel,
        grid=(NT_ALL,),
        in_specs=[row(Q_RANK), row(KVR_PAD), full(1, Q_RANK), full(1, KV_RANK),
                  full(Q_RANK, N_HEADS * QK_PAD), full(KV_RANK, N_HEADS * 256),
                  full(1, QK_PAD), full(1, QK_NOPE), full(1, 128),
                  row(128), row(128), row(128)],
        out_specs=[head(QK_PAD), head(QK_PAD), head(V_PAD)],
        out_shape=[jax.ShapeDtypeStruct((N_HEADS, T_ALL, QK_PAD), _BF16),
                   jax.ShapeDtypeStruct((N_HEADS, T_ALL, QK_PAD), _BF16),
                   jax.ShapeDtypeStruct((N_HEADS, T_ALL, V_PAD), _BF16)],
        compiler_params=_cparams(("parallel",)),
        name="mla_prep",
    )(pq, pkv, qag, kvag, wuq, wukv, qg, kgn, kgr, rope_c, rope_s1, rope_s2)


def _attn_kernel(q_ref, k_ref, v_ref, o_ref, sa_ref, sb_ref, acc_ref, *, n_keys, tk):
    q = q_ref[...]
    n_chunks = n_keys // tk
    assert n_chunks % 2 == 1

    def chunk(j):
        return pl.ds(j * tk if isinstance(j, int) else pl.multiple_of(j * tk, tk), tk)

    def scores(j, s_ref, m_cur):
        s = lax.dot_general(q, k_ref[chunk(j), :], (((1,), (1,)), ((), ())), preferred_element_type=_F32)
        s_ref[...] = s
        m = jnp.max(s, axis=-1, keepdims=True)
        return m if m_cur is None else jnp.maximum(m_cur, m)

    def accumulate(j, s_ref, m_prev, m_cur):
        p = jnp.exp2(s_ref[...] - m_cur)
        pv = jnp.dot(p.astype(_BF16), v_ref[chunk(j), :], preferred_element_type=_F32)
        acc_ref[...] = jnp.exp2(m_prev - m_cur) * acc_ref[...] + pv

    def step(i, carry):
        m_prev, m_cur = carry
        j = 2 * i
        m_1 = scores(j + 1, sb_ref, m_cur)
        accumulate(j, sa_ref, m_prev, m_cur)
        m_2 = scores(j + 2, sa_ref, m_1)
        accumulate(j + 1, sb_ref, m_cur, m_1)
        return m_1, m_2

    m0 = scores(0, sa_ref, None)
    acc_ref[...] = jnp.zeros_like(acc_ref)
    m_prev, m_cur = lax.fori_loop(0, (n_chunks - 1) // 2, step, (m0, m0))
    accumulate(n_chunks - 1, sa_ref, m_prev, m_cur)
    acc = acc_ref[...]
    o_ref[...] = acc[:, 0:V_DIM] / acc[:, V_DIM:V_DIM + 1]


def _attention(q, k, v, *, q_tile0, n_q, tq, key_tile0, n_keys, tk):
    return pl.pallas_call(
        functools.partial(_attn_kernel, n_keys=n_keys, tk=tk),
        grid=(N_HEADS, n_q // tq),
        in_specs=[
            pl.BlockSpec((None, tq, QK_PAD), lambda h, i: (h, q_tile0 + i, 0)),
            pl.BlockSpec((None, n_keys, QK_PAD), lambda h, i: (h, key_tile0, 0)),
            pl.BlockSpec((None, n_keys, V_PAD), lambda h, i: (h, key_tile0, 0)),
        ],
        out_specs=pl.BlockSpec((tq, V_DIM), lambda h, i: (i, h)),
        out_shape=jax.ShapeDtypeStruct((n_q, D_ATTN), _F32),
        scratch_shapes=[pltpu.VMEM((tq, tk), _F32), pltpu.VMEM((tq, tk), _F32), pltpu.VMEM((tq, V_PAD), _F32)],
        compiler_params=_cparams(("parallel", "parallel")),
        name="attention",
    )(q, k, v)


def _gelu_tanh(x):
    return 0.5 * x * (1.0 + jnp.tanh(math.sqrt(2.0 / math.pi) * (x + 0.044715 * (x * x * x))))


def _scan_tile(a, b, carry, reverse):
    rows = lax.broadcasted_iota(jnp.int32, (TM, 1), 0) % 8
    for s in (1, 2, 4):
        if reverse:
            a_sh, b_sh = pltpu.roll(a, TM - s, axis=0), pltpu.roll(b, TM - s, axis=0)
            ok = rows < 8 - s
        else:
            a_sh, b_sh = pltpu.roll(a, s, axis=0), pltpu.roll(b, s, axis=0)
            ok = rows >= s
        b = jnp.where(ok, a * b_sh + b, b)
        a = jnp.where(ok, a * a_sh, a)
    out = [None] * (TM // 8)
    order = range(TM // 8 - 1, -1, -1) if reverse else range(TM // 8)
    for g in order:
        h = a[g * 8:(g + 1) * 8, :] * carry + b[g * 8:(g + 1) * 8, :]
        carry = h[0:1, :] if reverse else h[7:8, :]
        out[g] = h
    return jnp.concatenate(out, axis=0), carry


def _rnn_kernel(rx_ref, rxp_ref, rxn_ref, cw_ref, cb_ref, wa_ref, ba_ref, wx_ref, bx_ref, lam_ref,
                *rest, reverse, final):
    if final:
        hf_ref, rg_ref, g_ref, o_ref, carry_ref = rest
    else:
        o_ref, carry_ref = rest
    j = pl.program_id(0)
    tile = jnp.where(j == 0, CTX_TILE, (NT_LAT - j) if reverse else (j - 1))

    @pl.when(j == 0)
    def _():
        carry_ref[...] = jnp.zeros_like(carry_ref)

    has_prev = jnp.logical_and(tile >= 1, tile < NT_LAT).astype(_F32)
    has_next = (tile < NT_LAT - 1).astype(_F32)
    cur = rx_ref[...]
    p6 = rxp_ref[HALO - 2:HALO - 1, :] * has_prev
    p7 = rxp_ref[HALO - 1:HALO, :] * has_prev
    n0 = rxn_ref[0:1, :] * has_next
    rows = lax.broadcasted_iota(jnp.int32, (TM, 1), 0)
    um2 = jnp.where(rows == 0, p6, jnp.where(rows == 1, p7, pltpu.roll(cur, 2, axis=0)))
    um1 = jnp.where(rows == 0, p7, pltpu.roll(cur, 1, axis=0))
    up1 = jnp.where(rows == TM - 1, n0, pltpu.roll(cur, TM - 1, axis=0))
    u = (cw_ref[0:1, :] * um2 + cw_ref[1:2, :] * um1 + cw_ref[2:3, :] * cur + cw_ref[3:4, :] * up1
         + cb_ref[...])

    ub = u.astype(_BF16)
    r_parts, i_parts = [], []
    for h in range(RNN_BLOCKS):
        uh = ub[:, h * RNN_BLOCK:(h + 1) * RNN_BLOCK]
        r_parts.append(jnp.dot(uh, wa_ref[h], preferred_element_type=_F32))
        i_parts.append(jnp.dot(uh, wx_ref[h], preferred_element_type=_F32))
    r = jax.nn.sigmoid(jnp.concatenate(r_parts, axis=-1) + ba_ref[...])
    i = jax.nn.sigmoid(jnp.concatenate(i_parts, axis=-1) + bx_ref[...])
    nl = -lam_ref[...]
    softplus = jnp.maximum(nl, 0.0) + jnp.log1p(jnp.exp(-jnp.abs(nl)))
    a = jnp.exp((-LRU_C) * r * softplus)
    b = jnp.sqrt(1.0 - a * a) * (i * u)

    h, carry = _scan_tile(a, b, carry_ref[0:1, :], reverse)
    carry_ref[0:1, :] = carry
    if final:
        rnn = _gelu_tanh(rg_ref[...]) * (hf_ref[...] + h)
        o_ref[...] = _rms(rnn, g_ref[...])
    else:
        o_ref[...] = h


def _rnn(l, d, rx, conv_w, conv_b, wa, ba, wx, bx, lam, extra=None):
    reverse = d == 1
    final = extra is not None

    def tile_of(j):
        return jnp.where(j == 0, CTX_TILE, (NT_LAT - j) if reverse else (j - 1))

    n8 = TM // HALO
    row = pl.BlockSpec((TM, D_RNN), lambda j: (tile_of(j), 0))
    vec = lambda a: pl.BlockSpec((None, None, 1, D_RNN), lambda j: (l, d, 0, 0))
    mat = pl.BlockSpec((None, None, RNN_BLOCKS, RNN_BLOCK, RNN_BLOCK), lambda j: (l, d, 0, 0, 0))
    in_specs = [
        row,
        pl.BlockSpec((HALO, D_RNN), lambda j: (jnp.maximum(tile_of(j) * n8 - 1, 0), 0)),
        pl.BlockSpec((HALO, D_RNN), lambda j: (jnp.minimum((tile_of(j) + 1) * n8, T_ALL // HALO - 1), 0)),
        pl.BlockSpec((None, CONV_W, D_RNN), lambda j: (l, 0, 0)),
        pl.BlockSpec((None, 1, D_RNN), lambda j: (l, 0, 0)),
        mat, vec(ba), mat, vec(bx), vec(lam),
    ]
    args = [rx, rx, rx, conv_w, conv_b.reshape(DEPTH, 1, D_RNN), wa, ba.reshape(DEPTH, 2, 1, D_RNN),
            wx, bx.reshape(DEPTH, 2, 1, D_RNN), lam.reshape(DEPTH, 2, 1, D_RNN)]
    if final:
        hf, rg, g = extra
        in_specs += [row, row, pl.BlockSpec((1, D_RNN), lambda j: (0, 0))]
        args += [hf, rg, g]
    return pl.pallas_call(
        functools.partial(_rnn_kernel, reverse=reverse, final=final),
        grid=(NT_ALL,),
        in_specs=in_specs,
        out_specs=row,
        out_shape=jax.ShapeDtypeStruct((T_ALL, D_RNN), _F32),
        scratch_shapes=[pltpu.VMEM((8, D_RNN), _F32)],
        compiler_params=_cparams(("arbitrary",)),
        name="rnn_bwd" if reverse else "rnn_fwd",
    )(*args)


def _out_proj_kernel(alat_ref, actx_ref, yr_ref, ag_ref, w_ref, x_ref, mod_ref, g2_ref, wr_ref, br_ref,
                     xo_ref, h2_ref, ri_ref, rw_ref, cnt_ref, carry_ref):
    i = pl.program_id(0)

    @pl.when(i == 0)
    def _():
        carry_ref[...] = jnp.zeros_like(carry_ref)

    a = jnp.where(i == CTX_TILE, actx_ref[...], alat_ref[...])
    ya = _rms(a, ag_ref[...]).astype(_BF16)
    yr = yr_ref[...].astype(_BF16)
    mix = (jnp.dot(ya, w_ref[0:D_ATTN, :], preferred_element_type=_F32)
           + jnp.dot(yr, w_ref[D_ATTN:D_MODEL, :], preferred_element_type=_F32))
    x = x_ref[...] + mod_ref[2:3, :] * mix
    xo_ref[...] = x
    h2 = _rms(x, g2_ref[...]) * (1.0 + mod_ref[4:5, :]) + mod_ref[3:4, :]
    h2_ref[...] = h2

    h2_hi = h2.astype(_BF16)
    h2_lo = (h2 - h2_hi.astype(_F32)).astype(_BF16)
    hh = jnp.dot(h2_hi, wr_ref[...], preferred_element_type=_F32)
    lh = jnp.dot(h2_lo, wr_ref[:, 0:ROUTE_LANES], preferred_element_type=_F32)
    logits = hh[:, 0:ROUTE_LANES] + hh[:, ROUTE_LANES:2 * ROUTE_LANES] + lh + br_ref[...]
    lane = lax.broadcasted_iota(jnp.int32, (TM, ROUTE_LANES), 1)
    lane_f = lane.astype(_F32)
    big = jnp.float32(1 << 20)
    neg = -jnp.inf
    gl = jnp.where(lane < N_GROUPS, logits, neg)
    gmax = jnp.max(gl, axis=-1, keepdims=True)
    gsum = jnp.sum(jnp.exp(gl - gmax), axis=-1, keepdims=True)
    p_g = 1.0 / gsum
    g_idx = jnp.min(jnp.where(gl == gmax, lane_f, big), axis=-1, keepdims=True).astype(jnp.int32)
    lo = E_OFF + EXPERTS_PER_GROUP * g_idx
    el = jnp.where(jnp.logical_and(lane >= lo, lane < lo + EXPERTS_PER_GROUP), logits, neg)
    m1 = jnp.max(el, axis=-1, keepdims=True)
    i1 = jnp.min(jnp.where(el == m1, lane_f, big), axis=-1, keepdims=True).astype(jnp.int32)
    el2 = jnp.where(lane == i1, neg, el)
    m2 = jnp.max(el2, axis=-1, keepdims=True)
    i2 = jnp.min(jnp.where(el2 == m2, lane_f, big), axis=-1, keepdims=True).astype(jnp.int32)
    e21 = jnp.exp(m2 - m1)
    w1 = p_g / (1.0 + e21)
    w2 = p_g * e21 / (1.0 + e21)

    hot1 = lane == i1
    hot2 = lane == i2
    hot = jnp.logical_or(hot1, hot2).astype(_F32)
    rr = lax.broadcasted_iota(jnp.int32, (TM, TM), 0)
    cc = lax.broadcasted_iota(jnp.int32, (TM, TM), 1)
    tri = (cc < rr).astype(_BF16)
    before = jnp.dot(tri, hot.astype(_BF16), preferred_element_type=_F32) + carry_ref[0:1, :]
    rank1 = jnp.sum(jnp.where(hot1, before, 0.0), axis=-1, keepdims=True)
    rank2 = jnp.sum(jnp.where(hot2, before, 0.0), axis=-1, keepdims=True)
    carry = carry_ref[0:1, :] + jnp.sum(hot, axis=0, keepdims=True)
    carry_ref[0:1, :] = carry
    cnt_ref[...] = jnp.broadcast_to(carry, cnt_ref.shape).astype(jnp.int32)

    ri = jnp.where(lane == 0, (i1 - E_OFF).astype(_F32), jnp.where(lane == 1, (i2 - E_OFF).astype(_F32),
                   jnp.where(lane == 2, rank1, jnp.where(lane == 3, rank2, 0.0))))
    ri_ref[...] = ri.T[0:8, :].astype(jnp.int32)
    rw_ref[...] = jnp.where(lane == 0, w1, jnp.where(lane == 1, w2, 0.0))


def _out_proj(n_tiles, alat, actx, yr, ag, w, x, mod_l, g2, wr, br):
    row = lambda n: pl.BlockSpec((TM, n), lambda i: (i, 0))
    full = lambda a, b: pl.BlockSpec((a, b), lambda i: (0, 0))
    n_rows = n_tiles * TM
    return pl.pallas_call(
        _out_proj_kernel,
        grid=(n_tiles,),
        in_specs=[
            pl.BlockSpec((TM, D_ATTN), lambda i: (jnp.minimum(i, NT_LAT - 1), 0)),
            full(TM, D_ATTN),
            row(D_RNN), full(1, D_ATTN), full(D_MODEL, D_MODEL), row(D_MODEL),
            pl.BlockSpec((None, 6, D_MODEL), lambda i: (i // NT_LAT, 0, 0)),
            full(1, D_MODEL), full(D_MODEL, 2 * ROUTE_LANES), full(1, ROUTE_LANES),
        ],
        out_specs=[row(D_MODEL), row(D_MODEL), pl.BlockSpec((8, TM), lambda i: (0, i)), row(ROUTE_LANES),
                   full(8, ROUTE_LANES)],
        out_shape=[
            jax.ShapeDtypeStruct((n_rows, D_MODEL), _F32),
            jax.ShapeDtypeStruct((n_rows, D_MODEL), _F32),
            jax.ShapeDtypeStruct((8, n_rows), jnp.int32),
            jax.ShapeDtypeStruct((n_rows, ROUTE_LANES), _F32),
            jax.ShapeDtypeStruct((8, ROUTE_LANES), jnp.int32),
        ],
        scratch_shapes=[pltpu.VMEM((8, ROUTE_LANES), _F32)],
        compiler_params=_cparams(("arbitrary",)),
        name="out_proj",
    )(alat, actx, yr, ag, w, x, mod_l, g2, wr, br)


def _dispatch_kernel(dest_ref, h2_ref, xs_in_hbm, xs_hbm, sem, *, n_tok):
    del xs_in_hbm
    i = pl.program_id(0)

    def copy(t, k):
        return pltpu.make_async_copy(h2_ref.at[pl.ds(t, 1), :],
                                     xs_hbm.at[pl.ds(dest_ref[k * n_tok + i * TM + t], 1), :], sem)

    def issue(t, c):
        copy(t, 0).start()
        copy(t, 1).start()
        return c

    def drain(t, c):
        copy(t, 0).wait()
        copy(t, 1).wait()
        return c

    lax.fori_loop(0, TM, issue, 0)
    lax.fori_loop(0, TM, drain, 0)


def _dispatch(n_tiles, n_slots, dest_flat, h2):
    xs0 = jnp.zeros((n_slots, D_MODEL), _F32)
    return pl.pallas_call(
        functools.partial(_dispatch_kernel, n_tok=n_tiles * TM),
        grid_spec=pltpu.PrefetchScalarGridSpec(
            num_scalar_prefetch=1,
            grid=(n_tiles,),
            in_specs=[pl.BlockSpec((TM, D_MODEL), lambda i, d: (i, 0)), pl.BlockSpec(memory_space=pl.ANY)],
            out_specs=pl.BlockSpec(memory_space=pl.ANY),
            scratch_shapes=[pltpu.SemaphoreType.DMA(())],
        ),
        out_shape=jax.ShapeDtypeStruct((n_slots, D_MODEL), _F32),
        input_output_aliases={2: 0},
        compiler_params=_cparams(("arbitrary",)),
        name="moe_dispatch",
    )(dest_flat, h2, xs0)


def _expert_kernel(blk_e_ref, x_ref, wgu_ref, wd_ref, y_ref):
    del blk_e_ref
    gu = jnp.dot(x_ref[...], wgu_ref[...], preferred_element_type=_F32)
    g, u = gu[:, 0:D_EXPERT], gu[:, D_EXPERT:2 * D_EXPERT]
    act = g * jax.nn.sigmoid(g) * u
    y_ref[...] = jnp.dot(act, wd_ref[...], preferred_element_type=_F32)


def _experts(l, n_blocks, blk_e, xs, w_gate_up, w_down):
    return pl.pallas_call(
        _expert_kernel,
        grid_spec=pltpu.PrefetchScalarGridSpec(
            num_scalar_prefetch=1,
            grid=(n_blocks,),
            in_specs=[
                pl.BlockSpec((MOE_BLK, D_MODEL), lambda b, be: (b, 0)),
                pl.BlockSpec((None, None, D_MODEL, 2 * D_EXPERT), lambda b, be: (l, be[b], 0, 0)),
                pl.BlockSpec((None, None, D_EXPERT, D_MODEL), lambda b, be: (l, be[b], 0, 0)),
            ],
            out_specs=pl.BlockSpec((MOE_BLK, D_MODEL), lambda b, be: (b, 0)),
        ),
        out_shape=jax.ShapeDtypeStruct((n_blocks * MOE_BLK, D_MODEL), _F32),
        compiler_params=_cparams(("arbitrary",)),
        name="moe_experts",
    )(blk_e, xs, w_gate_up, w_down)


def _combine_kernel(dest_ref, ys_hbm, x_ref, rw_ref, mod_ref, o_ref, buf, sem, *, n_tok):
    i = pl.program_id(0)

    def copy(t, k):
        return pltpu.make_async_copy(ys_hbm.at[pl.ds(dest_ref[k * n_tok + i * TM + t], 1), :],
                                     buf.at[k, pl.ds(t, 1), :], sem)

    def issue(t, c):
        copy(t, 0).start()
        copy(t, 1).start()
        return c

    def drain(t, c):
        copy(t, 0).wait()
        copy(t, 1).wait()
        return c

    lax.fori_loop(0, TM, issue, 0)
    lax.fori_loop(0, TM, drain, 0)
    rw = rw_ref[...]
    f = buf[0] * rw[:, 0:1] + buf[1] * rw[:, 1:2]
    o_ref[...] = x_ref[...] + mod_ref[5:6, :] * f


def _combine(n_tiles, dest_flat, ys, x, rw, mod_l):
    return pl.pallas_call(
        functools.partial(_combine_kernel, n_tok=n_tiles * TM),
        grid_spec=pltpu.PrefetchScalarGridSpec(
            num_scalar_prefetch=1,
            grid=(n_tiles,),
            in_specs=[
                pl.BlockSpec(memory_space=pl.ANY),
                pl.BlockSpec((TM, D_MODEL), lambda i, d: (i, 0)),
                pl.BlockSpec((TM, ROUTE_LANES), lambda i, d: (i, 0)),
                pl.BlockSpec((None, 6, D_MODEL), lambda i, d: (i // NT_LAT, 0, 0)),
            ],
            out_specs=pl.BlockSpec((TM, D_MODEL), lambda i, d: (i, 0)),
            scratch_shapes=[pltpu.VMEM((2, TM, D_MODEL), _F32), pltpu.SemaphoreType.DMA(())],
        ),
        out_shape=jax.ShapeDtypeStruct((n_tiles * TM, D_MODEL), _F32),
        compiler_params=_cparams(("arbitrary",)),
        name="moe_combine",
    )(dest_flat, ys, x, rw, mod_l)


def _moe(l, n_tiles, h2, ri, rw, counts, x, mod_l, w_gate_up, w_down):
    n_tok = n_tiles * TM
    n_blocks = n_tok * 2 // MOE_BLK + N_EXPERTS
    cnt = counts[0, E_OFF:E_OFF + N_EXPERTS]
    padded = (cnt + MOE_BLK - 1) // MOE_BLK * MOE_BLK
    pad_end = jnp.cumsum(padded)
    pad_start = pad_end - padded
    dest = (pad_start[ri[0:2]] + ri[2:4]).reshape(-1).astype(jnp.int32)
    blk_start = jnp.arange(n_blocks, dtype=jnp.int32) * MOE_BLK
    blk_e = jnp.minimum(jnp.sum((pad_end[None, :] <= blk_start[:, None]).astype(jnp.int32), axis=1),
                        N_EXPERTS - 1).astype(jnp.int32)
    xs = _dispatch(n_tiles, n_blocks * MOE_BLK, dest, h2)
    ys = _experts(l, n_blocks, blk_e, xs, w_gate_up, w_down)
    return _combine(n_tiles, dest, ys, x, rw, mod_l)


def _rope_tables():
    pos = jnp.arange(SEQ, dtype=_F32)
    row = jnp.floor(pos / GRID_W)
    col = pos - row * GRID_W
    n_freq = QK_ROPE // 4
    inv = ROPE_THETA ** (-jnp.arange(n_freq, dtype=_F32) / n_freq)
    ang = jnp.concatenate([row[:, None] * inv, col[:, None] * inv], axis=-1)
    cos, sin = jnp.cos(ang), jnp.sin(ang)
    z32 = jnp.zeros((SEQ, 32), _F32)
    z64 = jnp.zeros((SEQ, 64), _F32)
    c = jnp.concatenate([cos, cos, jnp.ones((SEQ, 64), _F32)], axis=-1)
    s1 = jnp.concatenate([-sin, z32, z64], axis=-1)
    s2 = jnp.concatenate([z32, sin, z64], axis=-1)
    ctx_c = jnp.ones((CTX_LEN, 128), _F32)
    ctx_s = jnp.zeros((CTX_LEN, 128), _F32)
    return (jnp.concatenate([c, ctx_c]), jnp.concatenate([s1, ctx_s]), jnp.concatenate([s2, ctx_s]))


def _pad_head_gain(g):
    return jnp.pad(g, ((0, 0), (0, QK_PAD - QK_DIM)))[:, None, :]


def kernel(x, c, ctx, c_ctx, w_mod, b_mod, norm1_g, norm2_g, w_in, q_a_norm_g, kv_a_norm_g, w_uq, w_ukv, q_norm_g, k_norm_g, conv_w, conv_b, lru_wa, lru_ba, lru_wx, lru_bx, lru_lambda, attn_out_norm_g, rnn_out_norm_g, w_out, router_group_w, router_group_b, router_expert_w, router_expert_b, w_gate_up, w_down):
    L = DEPTH
    zc = jnp.zeros((L, D_MODEL, 64), _F32)
    w_in_p = jnp.concatenate([w_in[:, :, 0:832], zc, w_in[:, :, 832:2880]], axis=-1).astype(_BF16)
    w_uq_p = jnp.pad(w_uq.reshape(L, Q_RANK, N_HEADS, QK_DIM), ((0, 0), (0, 0), (0, 0), (0, QK_PAD - QK_DIM)))
    w_uq_p = w_uq_p.reshape(L, Q_RANK, N_HEADS * QK_PAD).astype(_BF16)
    w_ukv_b = w_ukv.astype(_BF16)
    w_out_b = w_out.astype(_BF16)
    qg_p = _pad_head_gain(q_norm_g)
    kgn = k_norm_g[:, None, 0:QK_NOPE]
    kgr = jnp.pad(k_norm_g[:, QK_NOPE:QK_DIM], ((0, 0), (0, 64)))[:, None, :]
    wr = jnp.pad(jnp.concatenate([router_group_w, router_expert_w], axis=-1),
                 ((0, 0), (0, 0), (0, ROUTE_LANES - N_GROUPS - N_EXPERTS)))
    wr_hi = wr.astype(_BF16)
    wr = jnp.concatenate([wr_hi, (wr - wr_hi.astype(_F32)).astype(_BF16)], axis=-1)
    br = jnp.pad(jnp.concatenate([router_group_b, router_expert_b], axis=-1),
                 ((0, 0), (0, ROUTE_LANES - N_GROUPS - N_EXPERTS)))[:, None, :]
    rope_c, rope_s1, rope_s2 = _rope_tables()

    cond_t = jnp.stack([c[0], c_ctx], axis=-1)
    mod = _adaln(cond_t, w_mod, b_mod).reshape(L, 2, 6, D_MODEL)

    xs = jnp.concatenate([x[0], ctx[0]], axis=0)
    for l in range(L):
        last = l == L - 1
        n_tiles = NT_LAT if last else NT_ALL
        pq, pkv, rx, rg = _proj_in(xs, mod[l], norm1_g[l][None, :], w_in_p[l])
        q, k, v = _mla_prep(pq, pkv, q_a_norm_g[l][None, :], kv_a_norm_g[l][None, :], w_uq_p[l], w_ukv_b[l],
                            qg_p[l], kgn[l], kgr[l], rope_c, rope_s1, rope_s2)
        attn_lat = _attention(q, k, v, q_tile0=0, n_q=SEQ, tq=TQ, key_tile0=0, n_keys=T_ALL, tk=TM * 3)
        if last:
            attn_ctx = attn_lat[0:TM]
        else:
            attn_ctx = _attention(q, k, v, q_tile0=CTX_TILE, n_q=CTX_LEN, tq=CTX_LEN, key_tile0=CTX_TILE,
                                  n_keys=CTX_LEN, tk=CTX_LEN)
        h_f = _rnn(l, 0, rx, conv_w, conv_b, lru_wa, lru_ba, lru_wx, lru_bx, lru_lambda)
        y_rnn = _rnn(l, 1, rx, conv_w, conv_b, lru_wa, lru_ba, lru_wx, lru_bx, lru_lambda,
                     extra=(h_f, rg, rnn_out_norm_g[l][None, :]))
        x_mid, h2, ri, rw, counts = _out_proj(n_tiles, attn_lat, attn_ctx, y_rnn, attn_out_norm_g[l][None, :],
                                              w_out_b[l], xs, mod[l], norm2_g[l][None, :], wr[l], br[l])
        xs = _moe(l, n_tiles, h2, ri, rw, counts, x_mid, mod[l], w_gate_up, w_down)
    return xs.reshape(1, SEQ, D_MODEL)
```

```python
import functools
import math

import jax
import jax.numpy as jnp
from jax import lax
from jax.experimental import pallas as pl
from jax.experimental.pallas import tpu as pltpu

D_MODEL = 2048
SEQ = 8192
CTX_LEN = 256
DEPTH = 4
GRID_W = 64
N_HEADS = 8
D_ATTN = 1024
V_DIM = 128
QK_NOPE = 128
QK_ROPE = 64
QK_DIM = 192
Q_RANK = 512
KV_RANK = 256
D_RNN = 1024
RNN_BLOCKS = 8
RNN_BLOCK = 128
CONV_W = 4
LRU_C = 8.0
N_GROUPS = 4
EXPERTS_PER_GROUP = 8
N_EXPERTS = 32
D_EXPERT = 512
ROPE_THETA = 10000.0
EPS = 1e-6

T_ALL = SEQ + CTX_LEN
TM = 256
NT_ALL = T_ALL // TM
NT_LAT = SEQ // TM
CTX_TILE = NT_LAT
HALO = 8
QK_PAD = 256
V_PAD = 256
KVR_PAD = 384
D_IN_PAD = Q_RANK + KVR_PAD + 2 * D_RNN
ROUTE_LANES = 128
E_OFF = N_GROUPS
MOE_BLK = 256
TQ = 1024
TK = 512
VMEM_LIMIT = 56 * 1024 * 1024

_F32 = jnp.float32
_BF16 = jnp.bfloat16


def _cparams(sem):
    return pltpu.CompilerParams(dimension_semantics=sem, vmem_limit_bytes=VMEM_LIMIT)


def _rms(x, g):
    return x * lax.rsqrt(jnp.mean(x * x, axis=-1, keepdims=True) + EPS) * g


def _adaln_kernel(cond_ref, w_ref, b_ref, o_ref):
    cond = cond_ref[...]
    s = cond * jax.nn.sigmoid(cond)
    w = w_ref[...]
    b = b_ref[...]
    o_ref[0:1, :] = jnp.sum(s[:, 0:1] * w, axis=0, keepdims=True) + b
    o_ref[1:2, :] = jnp.sum(s[:, 1:2] * w, axis=0, keepdims=True) + b


def _adaln(cond_t, w_mod, b_mod):
    tn = 1024
    n_col = 6 * D_MODEL // tn
    return pl.pallas_call(
        _adaln_kernel,
        grid=(DEPTH, n_col),
        in_specs=[
            pl.BlockSpec((D_MODEL, 2), lambda l, j: (0, 0)),
            pl.BlockSpec((None, D_MODEL, tn), lambda l, j: (l, 0, j)),
            pl.BlockSpec((None, 1, tn), lambda l, j: (l, 0, j)),
        ],
        out_specs=pl.BlockSpec((None, 2, tn), lambda l, j: (l, 0, j)),
        out_shape=jax.ShapeDtypeStruct((DEPTH, 2, 6 * D_MODEL), _F32),
        compiler_params=_cparams(("parallel", "parallel")),
        name="adaln",
    )(cond_t, w_mod, b_mod.reshape(DEPTH, 1, 6 * D_MODEL))


def _proj_in_kernel(xl_ref, xc_ref, mod_ref, g_ref, w_ref, pq_ref, pkv_ref, rx_ref, rg_ref):
    x = jnp.where(pl.program_id(0) == CTX_TILE, xc_ref[...], xl_ref[...])
    h = _rms(x, g_ref[...]) * (1.0 + mod_ref[1:2, :]) + mod_ref[0:1, :]
    hb = h.astype(_BF16)
    o0, o1, o2 = Q_RANK, Q_RANK + KVR_PAD, Q_RANK + KVR_PAD + D_RNN
    pq_ref[...] = jnp.dot(hb, w_ref[:, 0:o0], preferred_element_type=_F32)
    pkv_ref[...] = jnp.dot(hb, w_ref[:, o0:o1], preferred_element_type=_F32)
    rx_ref[...] = jnp.dot(hb, w_ref[:, o1:o2], preferred_element_type=_F32)
    rg_ref[...] = jnp.dot(hb, w_ref[:, o2:D_IN_PAD], preferred_element_type=_F32)


def _lat_ctx_specs(ctx_blk):
    return [pl.BlockSpec((TM, D_MODEL), lambda i: (jnp.minimum(i, NT_LAT - 1), 0)),
            pl.BlockSpec((TM, D_MODEL), lambda i: (ctx_blk, 0))]


def _proj_in(x_lat, x_ctx, ctx_blk, mod_l, g, w):
    row = lambda n: pl.BlockSpec((TM, n), lambda i: (i, 0))
    return pl.pallas_call(
        _proj_in_kernel,
        grid=(NT_ALL,),
        in_specs=_lat_ctx_specs(ctx_blk) + [
            pl.BlockSpec((None, 6, D_MODEL), lambda i: (i // NT_LAT, 0, 0)),
            pl.BlockSpec((1, D_MODEL), lambda i: (0, 0)),
            pl.BlockSpec((D_MODEL, D_IN_PAD), lambda i: (0, 0)),
        ],
        out_specs=[row(Q_RANK), row(KVR_PAD), row(D_RNN), row(D_RNN)],
        out_shape=[
            jax.ShapeDtypeStruct((T_ALL, Q_RANK), _F32),
            jax.ShapeDtypeStruct((T_ALL, KVR_PAD), _F32),
            jax.ShapeDtypeStruct((T_ALL, D_RNN), _F32),
            jax.ShapeDtypeStruct((T_ALL, D_RNN), _F32),
        ],
        compiler_params=_cparams(("parallel",)),
        name="proj_in",
    )(x_lat, x_ctx, mod_l, g, w)


def _rope(y, c, s1, s2):
    return y * c + pltpu.roll(y, 96, axis=1) * s1 + pltpu.roll(y, 32, axis=1) * s2


def _mla_prep_kernel(pq_ref, pkv_ref, qag_ref, kvag_ref, wuq_ref, wukv_ref, qg_ref, kgn_ref, kgr_ref,
                     c_ref, s1_ref, s2_ref, q_ref, k_ref, v_ref):
    c, s1, s2 = c_ref[...], s1_ref[...], s2_ref[...]
    scale = QK_DIM ** -0.5 * math.log2(math.e)
    qa = _rms(pq_ref[...], qag_ref[...]).astype(_BF16)
    qf = jnp.dot(qa, wuq_ref[...], preferred_element_type=_F32)
    pkv = pkv_ref[...]
    kva = _rms(pkv[:, 0:KV_RANK], kvag_ref[...]).astype(_BF16)
    kvf = jnp.dot(kva, wukv_ref[...], preferred_element_type=_F32)
    kr = pkv[:, KV_RANK:KVR_PAD]
    kr_ss = jnp.sum(kr * kr, axis=-1, keepdims=True)
    kr_roped = _rope(kr * kgr_ref[...], c, s1, s2)
    qg = qg_ref[...]
    lane = lax.broadcasted_iota(jnp.int32, (TM, V_PAD - V_DIM), 1)
    ones_col = jnp.where(lane == 0, 1.0, 0.0).astype(_BF16)
    for h in range(N_HEADS):
        qh = qf[:, h * QK_PAD:(h + 1) * QK_PAD]
        r = lax.rsqrt(jnp.sum(qh * qh, axis=-1, keepdims=True) * (1.0 / QK_DIM) + EPS)
        qn = qh * r * qg
        q_ref[h, :, 0:QK_NOPE] = (qn[:, 0:QK_NOPE] * scale).astype(_BF16)
        q_ref[h, :, QK_NOPE:QK_PAD] = (_rope(qn[:, QK_NOPE:QK_PAD], c, s1, s2) * scale).astype(_BF16)
        kn = kvf[:, h * 256:h * 256 + QK_NOPE]
        rk = lax.rsqrt((jnp.sum(kn * kn, axis=-1, keepdims=True) + kr_ss) * (1.0 / QK_DIM) + EPS)
        k_ref[h, :, 0:QK_NOPE] = (kn * rk * kgn_ref[...]).astype(_BF16)
        k_ref[h, :, QK_NOPE:QK_PAD] = (kr_roped * rk).astype(_BF16)
        v_ref[h, :, 0:V_DIM] = kvf[:, h * 256 + QK_NOPE:(h + 1) * 256].astype(_BF16)
        v_ref[h, :, V_DIM:V_PAD] = ones_col


def _mla_prep(pq, pkv, qag, kvag, wuq, wukv, qg, kgn, kgr, rope_c, rope_s1, rope_s2):
    row = lambda n: pl.BlockSpec((TM, n), lambda i: (i, 0))
    full = lambda a, b: pl.BlockSpec((a, b), lambda i: (0, 0))
    head = lambda n: pl.BlockSpec((N_HEADS, TM, n), lambda i: (0, i, 0))
    return pl.pallas_call(
        _mla_prep_kernel,
        grid=(NT_ALL,),
        in_specs=[row(Q_RANK), row(KVR_PAD), full(1, Q_RANK), full(1, KV_RANK),
                  full(Q_RANK, N_HEADS * QK_PAD), full(KV_RANK, N_HEADS * 256),
                  full(1, QK_PAD), full(1, QK_NOPE), full(1, 128),
                  row(128), row(128), row(128)],
        out_specs=[head(QK_PAD), head(QK_PAD), head(V_PAD)],
        out_shape=[jax.ShapeDtypeStruct((N_HEADS, T_ALL, QK_PAD), _BF16),
                   jax.ShapeDtypeStruct((N_HEADS, T_ALL, QK_PAD), _BF16),
                   jax.ShapeDtypeStruct((N_HEADS, T_ALL, V_PAD), _BF16)],
        compiler_params=_cparams(("parallel",)),
        name="mla_prep",
    )(pq, pkv, qag, kvag, wuq, wukv, qg, kgn, kgr, rope_c, rope_s1, rope_s2)


def _attn_kernel(q_ref, k_ref, v_ref, o_ref, sa_ref, sb_ref, acc_ref, *, n_keys, tk):
    q = q_ref[...]
    n_chunks = n_keys // tk
    assert n_chunks % 2 == 1

    def chunk(j):
        return pl.ds(j * tk if isinstance(j, int) else pl.multiple_of(j * tk, tk), tk)

    def scores(j, s_ref, m_cur):
        s = lax.dot_general(q, k_ref[chunk(j), :], (((1,), (1,)), ((), ())), preferred_element_type=_F32)
        s_ref[...] = s
        m = jnp.max(s, axis=-1, keepdims=True)
        return m if m_cur is None else jnp.maximum(m_cur, m)

    def accumulate(j, s_ref, m_prev, m_cur):
        p = jnp.exp2(s_ref[...] - m_cur)
        pv = jnp.dot(p.astype(_BF16), v_ref[chunk(j), :], preferred_element_type=_F32)
        acc_ref[...] = jnp.exp2(m_prev - m_cur) * acc_ref[...] + pv

    def step(i, carry):
        m_prev, m_cur = carry
        j = 2 * i
        m_1 = scores(j + 1, sb_ref, m_cur)
        accumulate(j, sa_ref, m_prev, m_cur)
        m_2 = scores(j + 2, sa_ref, m_1)
        accumulate(j + 1, sb_ref, m_cur, m_1)
        return m_1, m_2

    m0 = scores(0, sa_ref, None)
    acc_ref[...] = jnp.zeros_like(acc_ref)
    m_prev, m_cur = lax.fori_loop(0, (n_chunks - 1) // 2, step, (m0, m0), unroll=True)
    accumulate(n_chunks - 1, sa_ref, m_prev, m_cur)
    acc = acc_ref[...]
    o_ref[...] = acc[:, 0:V_DIM] / acc[:, V_DIM:V_DIM + 1]


def _attention(q, k, v, *, q_tile0, n_q, tq, key_tile0, n_keys, tk):
    return pl.pallas_call(
        functools.partial(_attn_kernel, n_keys=n_keys, tk=tk),
        grid=(N_HEADS, n_q // tq),
        in_specs=[
            pl.BlockSpec((None, tq, QK_PAD), lambda h, i: (h, q_tile0 + i, 0)),
            pl.BlockSpec((None, n_keys, QK_PAD), lambda h, i: (h, key_tile0, 0)),
            pl.BlockSpec((None, n_keys, V_PAD), lambda h, i: (h, key_tile0, 0)),
        ],
        out_specs=pl.BlockSpec((tq, V_DIM), lambda h, i: (i, h)),
        out_shape=jax.ShapeDtypeStruct((n_q, D_ATTN), _F32),
        scratch_shapes=[pltpu.VMEM((tq, tk), _F32), pltpu.VMEM((tq, tk), _F32), pltpu.VMEM((tq, V_PAD), _F32)],
        compiler_params=_cparams(("parallel", "parallel")),
        name="attention",
    )(q, k, v)


def _gelu_tanh(x):
    return 0.5 * x * (1.0 + jnp.tanh(math.sqrt(2.0 / math.pi) * (x + 0.044715 * (x * x * x))))


def _scan_tile(a, b, carry, reverse):
    rows = lax.broadcasted_iota(jnp.int32, (TM, 1), 0) % 8
    for s in (1, 2, 4):
        if reverse:
            a_sh, b_sh = pltpu.roll(a, TM - s, axis=0), pltpu.roll(b, TM - s, axis=0)
            ok = rows < 8 - s
        else:
            a_sh, b_sh = pltpu.roll(a, s, axis=0), pltpu.roll(b, s, axis=0)
            ok = rows >= s
        b = jnp.where(ok, a * b_sh + b, b)
        a = jnp.where(ok, a * a_sh, a)
    out = [None] * (TM // 8)
    order = range(TM // 8 - 1, -1, -1) if reverse else range(TM // 8)
    for g in order:
        h = a[g * 8:(g + 1) * 8, :] * carry + b[g * 8:(g + 1) * 8, :]
        carry = h[0:1, :] if reverse else h[7:8, :]
        out[g] = h
    return jnp.concatenate(out, axis=0), carry


def _rnn_kernel(rx_ref, rxp_ref, rxn_ref, cw_ref, cb_ref, wa_ref, ba_ref, wx_ref, bx_ref, lam_ref,
                *rest, reverse, final):
    if final:
        hf_ref, rg_ref, g_ref, o_ref, carry_ref = rest
    else:
        o_ref, carry_ref = rest
    j = pl.program_id(0)
    tile = jnp.where(j == 0, CTX_TILE, (NT_LAT - j) if reverse else (j - 1))

    @pl.when(j == 0)
    def _():
        carry_ref[...] = jnp.zeros_like(carry_ref)

    has_prev = jnp.logical_and(tile >= 1, tile < NT_LAT).astype(_F32)
    has_next = (tile < NT_LAT - 1).astype(_F32)
    cur = rx_ref[...]
    p6 = rxp_ref[HALO - 2:HALO - 1, :] * has_prev
    p7 = rxp_ref[HALO - 1:HALO, :] * has_prev
    n0 = rxn_ref[0:1, :] * has_next
    rows = lax.broadcasted_iota(jnp.int32, (TM, 1), 0)
    um2 = jnp.where(rows == 0, p6, jnp.where(rows == 1, p7, pltpu.roll(cur, 2, axis=0)))
    um1 = jnp.where(rows == 0, p7, pltpu.roll(cur, 1, axis=0))
    up1 = jnp.where(rows == TM - 1, n0, pltpu.roll(cur, TM - 1, axis=0))
    u = (cw_ref[0:1, :] * um2 + cw_ref[1:2, :] * um1 + cw_ref[2:3, :] * cur + cw_ref[3:4, :] * up1
         + cb_ref[...])

    ub = u.astype(_BF16)
    r_parts, i_parts = [], []
    for h in range(RNN_BLOCKS):
        uh = ub[:, h * RNN_BLOCK:(h + 1) * RNN_BLOCK]
        r_parts.append(jnp.dot(uh, wa_ref[h], preferred_element_type=_F32))
        i_parts.append(jnp.dot(uh, wx_ref[h], preferred_element_type=_F32))
    r = jax.nn.sigmoid(jnp.concatenate(r_parts, axis=-1) + ba_ref[...])
    i = jax.nn.sigmoid(jnp.concatenate(i_parts, axis=-1) + bx_ref[...])
    nl = -lam_ref[...]
    softplus = jnp.maximum(nl, 0.0) + jnp.log1p(jnp.exp(-jnp.abs(nl)))
    a = jnp.exp((-LRU_C) * r * softplus)
    b = jnp.sqrt(1.0 - a * a) * (i * u)

    h, carry = _scan_tile(a, b, carry_ref[0:1, :], reverse)
    carry_ref[0:1, :] = carry
    if final:
        rnn = _gelu_tanh(rg_ref[...]) * (hf_ref[...] + h)
        o_ref[...] = _rms(rnn, g_ref[...])
    else:
        o_ref[...] = h


def _rnn(l, d, rx, conv_w, conv_b, wa, ba, wx, bx, lam, extra=None):
    reverse = d == 1
    final = extra is not None

    def tile_of(j):
        return jnp.where(j == 0, CTX_TILE, (NT_LAT - j) if reverse else (j - 1))

    n8 = TM // HALO
    row = pl.BlockSpec((TM, D_RNN), lambda j: (tile_of(j), 0))
    vec = lambda a: pl.BlockSpec((None, None, 1, D_RNN), lambda j: (l, d, 0, 0))
    mat = pl.BlockSpec((None, None, RNN_BLOCKS, RNN_BLOCK, RNN_BLOCK), lambda j: (l, d, 0, 0, 0))
    in_specs = [
        row,
        pl.BlockSpec((HALO, D_RNN), lambda j: (jnp.maximum(tile_of(j) * n8 - 1, 0), 0)),
        pl.BlockSpec((HALO, D_RNN), lambda j: (jnp.minimum((tile_of(j) + 1) * n8, T_ALL // HALO - 1), 0)),
        pl.BlockSpec((None, CONV_W, D_RNN), lambda j: (l, 0, 0)),
        pl.BlockSpec((None, 1, D_RNN), lambda j: (l, 0, 0)),
        mat, vec(ba), mat, vec(bx), vec(lam),
    ]
    args = [rx, rx, rx, conv_w, conv_b.reshape(DEPTH, 1, D_RNN), wa, ba.reshape(DEPTH, 2, 1, D_RNN),
            wx, bx.reshape(DEPTH, 2, 1, D_RNN), lam.reshape(DEPTH, 2, 1, D_RNN)]
    if final:
        hf, rg, g = extra
        in_specs += [row, row, pl.BlockSpec((1, D_RNN), lambda j: (0, 0))]
        args += [hf, rg, g]
    return pl.pallas_call(
        functools.partial(_rnn_kernel, reverse=reverse, final=final),
        grid=(NT_ALL,),
        in_specs=in_specs,
        out_specs=row,
        out_shape=jax.ShapeDtypeStruct((T_ALL, D_RNN), _F32),
        scratch_shapes=[pltpu.VMEM((8, D_RNN), _F32)],
        compiler_params=_cparams(("arbitrary",)),
        name="rnn_bwd" if reverse else "rnn_fwd",
    )(*args)


def _out_proj_kernel(alat_ref, actx_ref, yr_ref, ag_ref, w_ref, xl_ref, xc_ref, mod_ref, g2_ref, wr_ref, br_ref,
                     xo_ref, h2_ref, ri_ref, rw_ref, cnt_ref, carry_ref):
    i = pl.program_id(0)

    @pl.when(i == 0)
    def _():
        carry_ref[...] = jnp.zeros_like(carry_ref)

    a = jnp.where(i == CTX_TILE, actx_ref[...], alat_ref[...])
    ya = _rms(a, ag_ref[...]).astype(_BF16)
    yr = yr_ref[...].astype(_BF16)
    mix = (jnp.dot(ya, w_ref[0:D_ATTN, :], preferred_element_type=_F32)
           + jnp.dot(yr, w_ref[D_ATTN:D_MODEL, :], preferred_element_type=_F32))
    x = jnp.where(i == CTX_TILE, xc_ref[...], xl_ref[...]) + mod_ref[2:3, :] * mix
    xo_ref[...] = x
    h2 = _rms(x, g2_ref[...]) * (1.0 + mod_ref[4:5, :]) + mod_ref[3:4, :]
    h2_ref[...] = h2

    h2_hi = h2.astype(_BF16)
    h2_lo = (h2 - h2_hi.astype(_F32)).astype(_BF16)
    hh = jnp.dot(h2_hi, wr_ref[...], preferred_element_type=_F32)
    lh = jnp.dot(h2_lo, wr_ref[:, 0:ROUTE_LANES], preferred_element_type=_F32)
    logits = hh[:, 0:ROUTE_LANES] + hh[:, ROUTE_LANES:2 * ROUTE_LANES] + lh + br_ref[...]
    lane = lax.broadcasted_iota(jnp.int32, (TM, ROUTE_LANES), 1)
    lane_f = lane.astype(_F32)
    big = jnp.float32(1 << 20)
    neg = -jnp.inf
    gl = jnp.where(lane < N_GROUPS, logits, neg)
    gmax = jnp.max(gl, axis=-1, keepdims=True)
    gsum = jnp.sum(jnp.exp(gl - gmax), axis=-1, keepdims=True)
    p_g = 1.0 / gsum
    g_idx = jnp.min(jnp.where(gl == gmax, lane_f, big), axis=-1, keepdims=True).astype(jnp.int32)
    lo = E_OFF + EXPERTS_PER_GROUP * g_idx
    el = jnp.where(jnp.logical_and(lane >= lo, lane < lo + EXPERTS_PER_GROUP), logits, neg)
    m1 = jnp.max(el, axis=-1, keepdims=True)
    i1 = jnp.min(jnp.where(el == m1, lane_f, big), axis=-1, keepdims=True).astype(jnp.int32)
    el2 = jnp.where(lane == i1, neg, el)
    m2 = jnp.max(el2, axis=-1, keepdims=True)
    i2 = jnp.min(jnp.where(el2 == m2, lane_f, big), axis=-1, keepdims=True).astype(jnp.int32)
    e21 = jnp.exp(m2 - m1)
    w1 = p_g / (1.0 + e21)
    w2 = p_g * e21 / (1.0 + e21)

    hot1 = lane == i1
    hot2 = lane == i2
    hot = jnp.logical_or(hot1, hot2).astype(_F32)
    rr = lax.broadcasted_iota(jnp.int32, (TM, TM), 0)
    cc = lax.broadcasted_iota(jnp.int32, (TM, TM), 1)
    tri = (cc < rr).astype(_BF16)
    before = jnp.dot(tri, hot.astype(_BF16), preferred_element_type=_F32) + carry_ref[0:1, :]
    rank1 = jnp.sum(jnp.where(hot1, before, 0.0), axis=-1, keepdims=True)
    rank2 = jnp.sum(jnp.where(hot2, before, 0.0), axis=-1, keepdims=True)
    carry = carry_ref[0:1, :] + jnp.sum(hot, axis=0, keepdims=True)
    carry_ref[0:1, :] = carry
    cnt_ref[...] = jnp.broadcast_to(carry, cnt_ref.shape).astype(jnp.int32)

    ri = jnp.where(lane == 0, (i1 - E_OFF).astype(_F32), jnp.where(lane == 1, (i2 - E_OFF).astype(_F32),
                   jnp.where(lane == 2, rank1, jnp.where(lane == 3, rank2, 0.0))))
    ri_ref[...] = ri.T[0:4, :].astype(jnp.int32)
    rw_ref[...] = jnp.where(lane == 0, w1, jnp.where(lane == 1, w2, 0.0))


def _out_proj(n_tiles, alat, actx, yr, ag, w, x_lat, x_ctx, ctx_blk, mod_l, g2, wr, br):
    row = lambda n: pl.BlockSpec((TM, n), lambda i: (i, 0))
    full = lambda a, b: pl.BlockSpec((a, b), lambda i: (0, 0))
    n_rows = n_tiles * TM
    return pl.pallas_call(
        _out_proj_kernel,
        grid=(n_tiles,),
        in_specs=[
            pl.BlockSpec((TM, D_ATTN), lambda i: (jnp.minimum(i, NT_LAT - 1), 0)),
            full(TM, D_ATTN),
            row(D_RNN), full(1, D_ATTN), full(D_MODEL, D_MODEL), *_lat_ctx_specs(ctx_blk),
            pl.BlockSpec((None, 6, D_MODEL), lambda i: (i // NT_LAT, 0, 0)),
            full(1, D_MODEL), full(D_MODEL, 2 * ROUTE_LANES), full(1, ROUTE_LANES),
        ],
        out_specs=[row(D_MODEL), row(D_MODEL), pl.BlockSpec((4, TM), lambda i: (0, i)), row(ROUTE_LANES),
                   full(8, ROUTE_LANES)],
        out_shape=[
            jax.ShapeDtypeStruct((n_rows, D_MODEL), _F32),
            jax.ShapeDtypeStruct((n_rows, D_MODEL), _F32),
            jax.ShapeDtypeStruct((4, n_rows), jnp.int32),
            jax.ShapeDtypeStruct((n_rows, ROUTE_LANES), _F32),
            jax.ShapeDtypeStruct((8, ROUTE_LANES), jnp.int32),
        ],
        scratch_shapes=[pltpu.VMEM((8, ROUTE_LANES), _F32)],
        compiler_params=_cparams(("arbitrary",)),
        name="out_proj",
    )(alat, actx, yr, ag, w, x_lat, x_ctx, mod_l, g2, wr, br)


ROW_DMA_UNROLL = 8


def _row_dmas(n_tok, tile, start_ref, ri_ref, make_copy):
    def copy(t, k):
        tok = tile * TM + t
        slot = start_ref[ri_ref[k * n_tok + tok]] + ri_ref[(2 + k) * n_tok + tok]
        return make_copy(t, k, slot)

    def issue(t, c):
        copy(t, 0).start()
        copy(t, 1).start()
        return c

    def drain(t, c):
        copy(t, 0).wait()
        copy(t, 1).wait()
        return c

    lax.fori_loop(0, TM, issue, 0, unroll=ROW_DMA_UNROLL)
    lax.fori_loop(0, TM, drain, 0, unroll=ROW_DMA_UNROLL)


def _dispatch_kernel(start_ref, ri_ref, h2_ref, xs_in_hbm, xs_hbm, sem, *, n_tok):
    del xs_in_hbm
    _row_dmas(n_tok, pl.program_id(0), start_ref, ri_ref,
              lambda t, k, slot: pltpu.make_async_copy(h2_ref.at[pl.ds(t, 1), :],
                                                       xs_hbm.at[pl.ds(slot, 1), :], sem))


def _dispatch(n_tiles, n_slots, pad_start, ri_flat, h2):
    xs0 = jnp.zeros((n_slots, D_MODEL), _F32)
    return pl.pallas_call(
        functools.partial(_dispatch_kernel, n_tok=n_tiles * TM),
        grid_spec=pltpu.PrefetchScalarGridSpec(
            num_scalar_prefetch=2,
            grid=(n_tiles,),
            in_specs=[pl.BlockSpec((TM, D_MODEL), lambda i, ps, ri: (i, 0)), pl.BlockSpec(memory_space=pl.ANY)],
            out_specs=pl.BlockSpec(memory_space=pl.ANY),
            scratch_shapes=[pltpu.SemaphoreType.DMA(())],
        ),
        out_shape=jax.ShapeDtypeStruct((n_slots, D_MODEL), _F32),
        input_output_aliases={3: 0},
        compiler_params=_cparams(("arbitrary",)),
        name="moe_dispatch",
    )(pad_start, ri_flat, h2, xs0)


def _expert_kernel(blk_e_ref, n_used_ref, x_ref, wgu_ref, wd_ref, y_ref):
    del blk_e_ref
    used = pl.program_id(0) < n_used_ref[0]

    @pl.when(used)
    def _():
        gu = jnp.dot(x_ref[...], wgu_ref[...], preferred_element_type=_F32)
        g, u = gu[:, 0:D_EXPERT], gu[:, D_EXPERT:2 * D_EXPERT]
        act = g * jax.nn.sigmoid(g) * u
        y_ref[...] = jnp.dot(act, wd_ref[...], preferred_element_type=_F32)

    @pl.when(jnp.logical_not(used))
    def _():
        y_ref[...] = jnp.zeros_like(y_ref)


def _experts(l, n_blocks, blk_e, n_used, xs, w_gate_up, w_down):
    blk = lambda b, be, nu: (jnp.minimum(b, nu[0] - 1), 0)
    return pl.pallas_call(
        _expert_kernel,
        grid_spec=pltpu.PrefetchScalarGridSpec(
            num_scalar_prefetch=2,
            grid=(n_blocks,),
            in_specs=[
                pl.BlockSpec((MOE_BLK, D_MODEL), blk),
                pl.BlockSpec((None, None, D_MODEL, 2 * D_EXPERT), lambda b, be, nu: (l, be[b], 0, 0)),
                pl.BlockSpec((None, None, D_EXPERT, D_MODEL), lambda b, be, nu: (l, be[b], 0, 0)),
            ],
            out_specs=pl.BlockSpec((MOE_BLK, D_MODEL), lambda b, be, nu: (b, 0)),
        ),
        out_shape=jax.ShapeDtypeStruct((n_blocks * MOE_BLK, D_MODEL), _F32),
        compiler_params=_cparams(("arbitrary",)),
        name="moe_experts",
    )(blk_e, n_used, xs, w_gate_up, w_down)


def _combine_kernel(start_ref, ri_ref, ys_hbm, x_ref, rw_ref, mod_ref, o_ref, buf, sem, *, n_tok):
    _row_dmas(n_tok, pl.program_id(0), start_ref, ri_ref,
              lambda t, k, slot: pltpu.make_async_copy(ys_hbm.at[pl.ds(slot, 1), :],
                                                       buf.at[k, pl.ds(t, 1), :], sem))
    rw = rw_ref[...]
    f = buf[0] * rw[:, 0:1] + buf[1] * rw[:, 1:2]
    o_ref[...] = x_ref[...] + mod_ref[5:6, :] * f


def _combine(n_tiles, pad_start, ri_flat, ys, x, rw, mod_l):
    row = lambda n: pl.BlockSpec((TM, n), lambda i, ps, ri: (i, 0))
    return pl.pallas_call(
        functools.partial(_combine_kernel, n_tok=n_tiles * TM),
        grid_spec=pltpu.PrefetchScalarGridSpec(
            num_scalar_prefetch=2,
            grid=(n_tiles,),
            in_specs=[
                pl.BlockSpec(memory_space=pl.ANY),
                row(D_MODEL), row(ROUTE_LANES),
                pl.BlockSpec((None, 6, D_MODEL), lambda i, ps, ri: (i // NT_LAT, 0, 0)),
            ],
            out_specs=row(D_MODEL),
            scratch_shapes=[pltpu.VMEM((2, TM, D_MODEL), _F32), pltpu.SemaphoreType.DMA(())],
        ),
        out_shape=jax.ShapeDtypeStruct((n_tiles * TM, D_MODEL), _F32),
        compiler_params=_cparams(("arbitrary",)),
        name="moe_combine",
    )(pad_start, ri_flat, ys, x, rw, mod_l)


def _moe(l, n_tiles, h2, ri, rw, counts, x, mod_l, w_gate_up, w_down):
    n_tok = n_tiles * TM
    n_blocks = n_tok * 2 // MOE_BLK + N_EXPERTS
    cnt = counts[0, E_OFF:E_OFF + N_EXPERTS]
    padded = (cnt + MOE_BLK - 1) // MOE_BLK * MOE_BLK
    pad_end = jnp.cumsum(padded).astype(jnp.int32)
    pad_start = pad_end - padded
    blk_start = jnp.arange(n_blocks, dtype=jnp.int32) * MOE_BLK
    blk_e = jnp.minimum(jnp.sum((pad_end[None, :] <= blk_start[:, None]).astype(jnp.int32), axis=1),
                        N_EXPERTS - 1).astype(jnp.int32)
    n_used = pad_end[N_EXPERTS - 1:N_EXPERTS] // MOE_BLK
    ri_flat = ri.reshape(-1)
    xs = _dispatch(n_tiles, n_blocks * MOE_BLK, pad_start, ri_flat, h2)
    ys = _experts(l, n_blocks, blk_e, n_used, xs, w_gate_up, w_down)
    return _combine(n_tiles, pad_start, ri_flat, ys, x, rw, mod_l)


def _rope_tables():
    pos = jnp.arange(SEQ, dtype=_F32)
    row = jnp.floor(pos / GRID_W)
    col = pos - row * GRID_W
    n_freq = QK_ROPE // 4
    inv = ROPE_THETA ** (-jnp.arange(n_freq, dtype=_F32) / n_freq)
    ang = jnp.concatenate([row[:, None] * inv, col[:, None] * inv], axis=-1)
    cos, sin = jnp.cos(ang), jnp.sin(ang)
    z32 = jnp.zeros((SEQ, 32), _F32)
    z64 = jnp.zeros((SEQ, 64), _F32)
    c = jnp.concatenate([cos, cos, jnp.ones((SEQ, 64), _F32)], axis=-1)
    s1 = jnp.concatenate([-sin, z32, z64], axis=-1)
    s2 = jnp.concatenate([z32, sin, z64], axis=-1)
    ctx_c = jnp.ones((CTX_LEN, 128), _F32)
    ctx_s = jnp.zeros((CTX_LEN, 128), _F32)
    return (jnp.concatenate([c, ctx_c]), jnp.concatenate([s1, ctx_s]), jnp.concatenate([s2, ctx_s]))


def _pad_head_gain(g):
    return jnp.pad(g, ((0, 0), (0, QK_PAD - QK_DIM)))[:, None, :]


def kernel(x, c, ctx, c_ctx, w_mod, b_mod, norm1_g, norm2_g, w_in, q_a_norm_g, kv_a_norm_g, w_uq, w_ukv, q_norm_g, k_norm_g, conv_w, conv_b, lru_wa, lru_ba, lru_wx, lru_bx, lru_lambda, attn_out_norm_g, rnn_out_norm_g, w_out, router_group_w, router_group_b, router_expert_w, router_expert_b, w_gate_up, w_down):
    L = DEPTH
    zc = jnp.zeros((L, D_MODEL, 64), _F32)
    w_in_p = jnp.concatenate([w_in[:, :, 0:832], zc, w_in[:, :, 832:2880]], axis=-1).astype(_BF16)
    w_uq_p = jnp.pad(w_uq.reshape(L, Q_RANK, N_HEADS, QK_DIM), ((0, 0), (0, 0), (0, 0), (0, QK_PAD - QK_DIM)))
    w_uq_p = w_uq_p.reshape(L, Q_RANK, N_HEADS * QK_PAD).astype(_BF16)
    w_ukv_b = w_ukv.astype(_BF16)
    w_out_b = w_out.astype(_BF16)
    qg_p = _pad_head_gain(q_norm_g)
    kgn = k_norm_g[:, None, 0:QK_NOPE]
    kgr = jnp.pad(k_norm_g[:, QK_NOPE:QK_DIM], ((0, 0), (0, 64)))[:, None, :]
    wr = jnp.pad(jnp.concatenate([router_group_w, router_expert_w], axis=-1),
                 ((0, 0), (0, 0), (0, ROUTE_LANES - N_GROUPS - N_EXPERTS)))
    wr_hi = wr.astype(_BF16)
    wr = jnp.concatenate([wr_hi, (wr - wr_hi.astype(_F32)).astype(_BF16)], axis=-1)
    br = jnp.pad(jnp.concatenate([router_group_b, router_expert_b], axis=-1),
                 ((0, 0), (0, ROUTE_LANES - N_GROUPS - N_EXPERTS)))[:, None, :]
    rope_c, rope_s1, rope_s2 = _rope_tables()

    cond_t = jnp.stack([c[0], c_ctx], axis=-1)
    mod = _adaln(cond_t, w_mod, b_mod).reshape(L, 2, 6, D_MODEL)

    x_lat, x_ctx, ctx_blk = x[0], ctx[0], 0
    for l in range(L):
        last = l == L - 1
        n_tiles = NT_LAT if last else NT_ALL
        pq, pkv, rx, rg = _proj_in(x_lat, x_ctx, ctx_blk, mod[l], norm1_g[l][None, :], w_in_p[l])
        q, k, v = _mla_prep(pq, pkv, q_a_norm_g[l][None, :], kv_a_norm_g[l][None, :], w_uq_p[l], w_ukv_b[l],
                            qg_p[l], kgn[l], kgr[l], rope_c, rope_s1, rope_s2)
        attn_lat = _attention(q, k, v, q_tile0=0, n_q=SEQ, tq=TQ, key_tile0=0, n_keys=T_ALL, tk=TM * 3)
        if last:
            attn_ctx = attn_lat[0:TM]
        else:
            attn_ctx = _attention(q, k, v, q_tile0=CTX_TILE, n_q=CTX_LEN, tq=CTX_LEN, key_tile0=CTX_TILE,
                                  n_keys=CTX_LEN, tk=CTX_LEN)
        h_f = _rnn(l, 0, rx, conv_w, conv_b, lru_wa, lru_ba, lru_wx, lru_bx, lru_lambda)
        y_rnn = _rnn(l, 1, rx, conv_w, conv_b, lru_wa, lru_ba, lru_wx, lru_bx, lru_lambda,
                     extra=(h_f, rg, rnn_out_norm_g[l][None, :]))
        x_mid, h2, ri, rw, counts = _out_proj(n_tiles, attn_lat, attn_ctx, y_rnn, attn_out_norm_g[l][None, :],
                                              w_out_b[l], x_lat, x_ctx, ctx_blk, mod[l], norm2_g[l][None, :],
                                              wr[l], br[l])
        x_lat = _moe(l, n_tiles, h2, ri, rw, counts, x_mid, mod[l], w_gate_up, w_down)
        x_ctx, ctx_blk = x_lat, CTX_TILE
    return x_lat.reshape(1, SEQ, D_MODEL)
```

```python
import functools
import math

import jax
import jax.numpy as jnp
from jax import lax
from jax.experimental import pallas as pl
from jax.experimental.pallas import tpu as pltpu

D_MODEL = 2048
SEQ = 8192
CTX_LEN = 256
DEPTH = 4
GRID_W = 64
N_HEADS = 8
D_ATTN = 1024
V_DIM = 128
QK_NOPE = 128
QK_ROPE = 64
QK_DIM = 192
Q_RANK = 512
KV_RANK = 256
D_RNN = 1024
RNN_BLOCKS = 8
RNN_BLOCK = 128
CONV_W = 4
LRU_C = 8.0
N_GROUPS = 4
EXPERTS_PER_GROUP = 8
N_EXPERTS = 32
D_EXPERT = 512
ROPE_THETA = 10000.0
EPS = 1e-6

T_ALL = SEQ + CTX_LEN
TM = 256
NT_ALL = T_ALL // TM
NT_LAT = SEQ // TM
CTX_TILE = NT_LAT
HALO = 8
QK_PAD = 256
V_PAD = 256
KVR_PAD = 384
D_IN_PAD = Q_RANK + KVR_PAD + 2 * D_RNN
ROUTE_LANES = 128
E_OFF = N_GROUPS
MOE_BLK = 256
TQ = 1024
TK = 512
VMEM_LIMIT = 56 * 1024 * 1024

_F32 = jnp.float32
_BF16 = jnp.bfloat16


def _cparams(sem):
    return pltpu.CompilerParams(dimension_semantics=sem, vmem_limit_bytes=VMEM_LIMIT)


def _rms(x, g):
    return x * lax.rsqrt(jnp.mean(x * x, axis=-1, keepdims=True) + EPS) * g


def _adaln_kernel(cond_ref, w_ref, b_ref, o_ref):
    cond = cond_ref[...]
    s = cond * jax.nn.sigmoid(cond)
    w = w_ref[...]
    b = b_ref[...]
    o_ref[0:1, :] = jnp.sum(s[:, 0:1] * w, axis=0, keepdims=True) + b
    o_ref[1:2, :] = jnp.sum(s[:, 1:2] * w, axis=0, keepdims=True) + b


def _adaln(cond_t, w_mod, b_mod):
    tn = 1024
    n_col = 6 * D_MODEL // tn
    return pl.pallas_call(
        _adaln_kernel,
        grid=(DEPTH, n_col),
        in_specs=[
            pl.BlockSpec((D_MODEL, 2), lambda l, j: (0, 0)),
            pl.BlockSpec((None, D_MODEL, tn), lambda l, j: (l, 0, j)),
            pl.BlockSpec((None, 1, tn), lambda l, j: (l, 0, j)),
        ],
        out_specs=pl.BlockSpec((None, 2, tn), lambda l, j: (l, 0, j)),
        out_shape=jax.ShapeDtypeStruct((DEPTH, 2, 6 * D_MODEL), _F32),
        compiler_params=_cparams(("parallel", "parallel")),
        name="adaln",
    )(cond_t, w_mod, b_mod.reshape(DEPTH, 1, 6 * D_MODEL))


def _proj_in_kernel(xl_ref, xc_ref, mod_ref, g_ref, w_ref, pq_ref, pkv_ref, rx_ref, rg_ref):
    x = jnp.where(pl.program_id(0) == CTX_TILE, xc_ref[...], xl_ref[...])
    h = _rms(x, g_ref[...]) * (1.0 + mod_ref[1:2, :]) + mod_ref[0:1, :]
    hb = h.astype(_BF16)
    o0, o1, o2 = Q_RANK, Q_RANK + KVR_PAD, Q_RANK + KVR_PAD + D_RNN
    pq_ref[...] = jnp.dot(hb, w_ref[:, 0:o0], preferred_element_type=_F32)
    pkv_ref[...] = jnp.dot(hb, w_ref[:, o0:o1], preferred_element_type=_F32)
    rx_ref[...] = jnp.dot(hb, w_ref[:, o1:o2], preferred_element_type=_F32)
    rg_ref[...] = jnp.dot(hb, w_ref[:, o2:D_IN_PAD], preferred_element_type=_F32)


def _lat_ctx_specs(ctx_blk):
    return [pl.BlockSpec((TM, D_MODEL), lambda i: (jnp.minimum(i, NT_LAT - 1), 0)),
            pl.BlockSpec((TM, D_MODEL), lambda i: (ctx_blk, 0))]


def _proj_in(x_lat, x_ctx, ctx_blk, mod_l, g, w):
    row = lambda n: pl.BlockSpec((TM, n), lambda i: (i, 0))
    return pl.pallas_call(
        _proj_in_kernel,
        grid=(NT_ALL,),
        in_specs=_lat_ctx_specs(ctx_blk) + [
            pl.BlockSpec((None, 6, D_MODEL), lambda i: (i // NT_LAT, 0, 0)),
            pl.BlockSpec((1, D_MODEL), lambda i: (0, 0)),
            pl.BlockSpec((D_MODEL, D_IN_PAD), lambda i: (0, 0)),
        ],
        out_specs=[row(Q_RANK), row(KVR_PAD), row(D_RNN), row(D_RNN)],
        out_shape=[
            jax.ShapeDtypeStruct((T_ALL, Q_RANK), _F32),
            jax.ShapeDtypeStruct((T_ALL, KVR_PAD), _F32),
            jax.ShapeDtypeStruct((T_ALL, D_RNN), _F32),
            jax.ShapeDtypeStruct((T_ALL, D_RNN), _F32),
        ],
        compiler_params=_cparams(("parallel",)),
        name="proj_in",
    )(x_lat, x_ctx, mod_l, g, w)


def _rope(y, c, s1, s2):
    return y * c + pltpu.roll(y, 96, axis=1) * s1 + pltpu.roll(y, 32, axis=1) * s2


def _mla_prep_kernel(pq_ref, pkv_ref, qag_ref, kvag_ref, wuq_ref, wukv_ref, qg_ref, kgn_ref, kgr_ref,
                     c_ref, s1_ref, s2_ref, q_ref, k_ref, v_ref):
    c, s1, s2 = c_ref[...], s1_ref[...], s2_ref[...]
    scale = QK_DIM ** -0.5 * math.log2(math.e)
    qa = _rms(pq_ref[...], qag_ref[...]).astype(_BF16)
    qf = jnp.dot(qa, wuq_ref[...], preferred_element_type=_F32)
    pkv = pkv_ref[...]
    kva = _rms(pkv[:, 0:KV_RANK], kvag_ref[...]).astype(_BF16)
    kvf = jnp.dot(kva, wukv_ref[...], preferred_element_type=_F32)
    kr = pkv[:, KV_RANK:KVR_PAD]
    kr_ss = jnp.sum(kr * kr, axis=-1, keepdims=True)
    kr_roped = _rope(kr * kgr_ref[...], c, s1, s2)
    qg = qg_ref[...]
    lane = lax.broadcasted_iota(jnp.int32, (TM, V_PAD - V_DIM), 1)
    ones_col = jnp.where(lane == 0, 1.0, 0.0).astype(_BF16)
    for h in range(N_HEADS):
        qh = qf[:, h * QK_PAD:(h + 1) * QK_PAD]
        r = lax.rsqrt(jnp.sum(qh * qh, axis=-1, keepdims=True) * (1.0 / QK_DIM) + EPS)
        qn = qh * r * qg
        q_ref[h, :, 0:QK_NOPE] = (qn[:, 0:QK_NOPE] * scale).astype(_BF16)
        q_ref[h, :, QK_NOPE:QK_PAD] = (_rope(qn[:, QK_NOPE:QK_PAD], c, s1, s2) * scale).astype(_BF16)
        kn = kvf[:, h * 256:h * 256 + QK_NOPE]
        rk = lax.rsqrt((jnp.sum(kn * kn, axis=-1, keepdims=True) + kr_ss) * (1.0 / QK_DIM) + EPS)
        k_ref[h, :, 0:QK_NOPE] = (kn * rk * kgn_ref[...]).astype(_BF16)
        k_ref[h, :, QK_NOPE:QK_PAD] = (kr_roped * rk).astype(_BF16)
        v_ref[h, :, 0:V_DIM] = kvf[:, h * 256 + QK_NOPE:(h + 1) * 256].astype(_BF16)
        v_ref[h, :, V_DIM:V_PAD] = ones_col


def _mla_prep(pq, pkv, qag, kvag, wuq, wukv, qg, kgn, kgr, rope_c, rope_s1, rope_s2):
    row = lambda n: pl.BlockSpec((TM, n), lambda i: (i, 0))
    full = lambda a, b: pl.BlockSpec((a, b), lambda i: (0, 0))
    head = lambda n: pl.BlockSpec((N_HEADS, TM, n), lambda i: (0, i, 0))
    return pl.pallas_call(
        _mla_prep_kernel,
        grid=(NT_ALL,),
        in_specs=[row(Q_RANK), row(KVR_PAD), full(1, Q_RANK), full(1, KV_RANK),
                  full(Q_RANK, N_HEADS * QK_PAD), full(KV_RANK, N_HEADS * 256),
                  full(1, QK_PAD), full(1, QK_NOPE), full(1, 128),
                  row(128), row(128), row(128)],
        out_specs=[head(QK_PAD), head(QK_PAD), head(V_PAD)],
        out_shape=[jax.ShapeDtypeStruct((N_HEADS, T_ALL, QK_PAD), _BF16),
                   jax.ShapeDtypeStruct((N_HEADS, T_ALL, QK_PAD), _BF16),
                   jax.ShapeDtypeStruct((N_HEADS, T_ALL, V_PAD), _BF16)],
        compiler_params=_cparams(("parallel",)),
        name="mla_prep",
    )(pq, pkv, qag, kvag, wuq, wukv, qg, kgn, kgr, rope_c, rope_s1, rope_s2)


def _attn_kernel(q_ref, k_ref, v_ref, o_ref, sa_ref, sb_ref, acc_ref, *, n_keys, tk):
    q = q_ref[...]
    n_chunks = n_keys // tk
    assert n_chunks % 2 == 1

    def chunk(j):
        return pl.ds(j * tk if isinstance(j, int) else pl.multiple_of(j * tk, tk), tk)

    def scores(j, s_ref, m_cur):
        s = lax.dot_general(q, k_ref[chunk(j), :], (((1,), (1,)), ((), ())), preferred_element_type=_F32)
        s_ref[...] = s
        m = jnp.max(s, axis=-1, keepdims=True)
        return m if m_cur is None else jnp.maximum(m_cur, m)

    def accumulate(j, s_ref, m_prev, m_cur):
        p = jnp.exp2(s_ref[...] - m_cur)
        pv = jnp.dot(p.astype(_BF16), v_ref[chunk(j), :], preferred_element_type=_F32)
        acc_ref[...] = jnp.exp2(m_prev - m_cur) * acc_ref[...] + pv

    def step(i, carry):
        m_prev, m_cur = carry
        j = 2 * i
        m_1 = scores(j + 1, sb_ref, m_cur)
        accumulate(j, sa_ref, m_prev, m_cur)
        m_2 = scores(j + 2, sa_ref, m_1)
        accumulate(j + 1, sb_ref, m_cur, m_1)
        return m_1, m_2

    m0 = scores(0, sa_ref, None)
    acc_ref[...] = jnp.zeros_like(acc_ref)
    m_prev, m_cur = lax.fori_loop(0, (n_chunks - 1) // 2, step, (m0, m0), unroll=True)
    accumulate(n_chunks - 1, sa_ref, m_prev, m_cur)
    acc = acc_ref[...]
    o_ref[...] = acc[:, 0:V_DIM] / acc[:, V_DIM:V_DIM + 1]


def _attention(q, k, v, *, q_tile0, n_q, tq, key_tile0, n_keys, tk):
    return pl.pallas_call(
        functools.partial(_attn_kernel, n_keys=n_keys, tk=tk),
        grid=(N_HEADS, n_q // tq),
        in_specs=[
            pl.BlockSpec((None, tq, QK_PAD), lambda h, i: (h, q_tile0 + i, 0)),
            pl.BlockSpec((None, n_keys, QK_PAD), lambda h, i: (h, key_tile0, 0)),
            pl.BlockSpec((None, n_keys, V_PAD), lambda h, i: (h, key_tile0, 0)),
        ],
        out_specs=pl.BlockSpec((tq, V_DIM), lambda h, i: (i, h)),
        out_shape=jax.ShapeDtypeStruct((n_q, D_ATTN), _F32),
        scratch_shapes=[pltpu.VMEM((tq, tk), _F32), pltpu.VMEM((tq, tk), _F32), pltpu.VMEM((tq, V_PAD), _F32)],
        compiler_params=_cparams(("parallel", "parallel")),
        name="attention",
    )(q, k, v)


def _gelu_tanh(x):
    return 0.5 * x * (1.0 + jnp.tanh(math.sqrt(2.0 / math.pi) * (x + 0.044715 * (x * x * x))))


def _scan_tile(a, b, carry, reverse):
    rows = lax.broadcasted_iota(jnp.int32, (TM, 1), 0) % 8
    for s in (1, 2, 4):
        if reverse:
            a_sh, b_sh = pltpu.roll(a, TM - s, axis=0), pltpu.roll(b, TM - s, axis=0)
            ok = rows < 8 - s
        else:
            a_sh, b_sh = pltpu.roll(a, s, axis=0), pltpu.roll(b, s, axis=0)
            ok = rows >= s
        b = jnp.where(ok, a * b_sh + b, b)
        a = jnp.where(ok, a * a_sh, a)
    out = [None] * (TM // 8)
    order = range(TM // 8 - 1, -1, -1) if reverse else range(TM // 8)
    for g in order:
        h = a[g * 8:(g + 1) * 8, :] * carry + b[g * 8:(g + 1) * 8, :]
        carry = h[0:1, :] if reverse else h[7:8, :]
        out[g] = h
    return jnp.concatenate(out, axis=0), carry


def _rnn_kernel(rx_ref, rxp_ref, rxn_ref, cw_ref, cb_ref, wa_ref, ba_ref, wx_ref, bx_ref, lam_ref,
                *rest, reverse, final):
    if final:
        hf_ref, rg_ref, g_ref, o_ref, carry_ref = rest
    else:
        o_ref, carry_ref = rest
    j = pl.program_id(0)
    tile = jnp.where(j == 0, CTX_TILE, (NT_LAT - j) if reverse else (j - 1))

    @pl.when(j == 0)
    def _():
        carry_ref[...] = jnp.zeros_like(carry_ref)

    has_prev = jnp.logical_and(tile >= 1, tile < NT_LAT).astype(_F32)
    has_next = (tile < NT_LAT - 1).astype(_F32)
    cur = rx_ref[...]
    p6 = rxp_ref[HALO - 2:HALO - 1, :] * has_prev
    p7 = rxp_ref[HALO - 1:HALO, :] * has_prev
    n0 = rxn_ref[0:1, :] * has_next
    rows = lax.broadcasted_iota(jnp.int32, (TM, 1), 0)
    um2 = jnp.where(rows == 0, p6, jnp.where(rows == 1, p7, pltpu.roll(cur, 2, axis=0)))
    um1 = jnp.where(rows == 0, p7, pltpu.roll(cur, 1, axis=0))
    up1 = jnp.where(rows == TM - 1, n0, pltpu.roll(cur, TM - 1, axis=0))
    u = (cw_ref[0:1, :] * um2 + cw_ref[1:2, :] * um1 + cw_ref[2:3, :] * cur + cw_ref[3:4, :] * up1
         + cb_ref[...])

    ub = u.astype(_BF16)
    r_parts, i_parts = [], []
    for h in range(RNN_BLOCKS):
        uh = ub[:, h * RNN_BLOCK:(h + 1) * RNN_BLOCK]
        r_parts.append(jnp.dot(uh, wa_ref[h], preferred_element_type=_F32))
        i_parts.append(jnp.dot(uh, wx_ref[h], preferred_element_type=_F32))
    r = jax.nn.sigmoid(jnp.concatenate(r_parts, axis=-1) + ba_ref[...])
    i = jax.nn.sigmoid(jnp.concatenate(i_parts, axis=-1) + bx_ref[...])
    nl = -lam_ref[...]
    softplus = jnp.maximum(nl, 0.0) + jnp.log1p(jnp.exp(-jnp.abs(nl)))
    a = jnp.exp((-LRU_C) * r * softplus)
    b = jnp.sqrt(1.0 - a * a) * (i * u)

    h, carry = _scan_tile(a, b, carry_ref[0:1, :], reverse)
    carry_ref[0:1, :] = carry
    if final:
        rnn = _gelu_tanh(rg_ref[...]) * (hf_ref[...] + h)
        o_ref[...] = _rms(rnn, g_ref[...])
    else:
        o_ref[...] = h


def _rnn(l, d, rx, conv_w, conv_b, wa, ba, wx, bx, lam, extra=None):
    reverse = d == 1
    final = extra is not None

    def tile_of(j):
        return jnp.where(j == 0, CTX_TILE, (NT_LAT - j) if reverse else (j - 1))

    n8 = TM // HALO
    row = pl.BlockSpec((TM, D_RNN), lambda j: (tile_of(j), 0))
    vec = lambda a: pl.BlockSpec((None, None, 1, D_RNN), lambda j: (l, d, 0, 0))
    mat = pl.BlockSpec((None, None, RNN_BLOCKS, RNN_BLOCK, RNN_BLOCK), lambda j: (l, d, 0, 0, 0))
    in_specs = [
        row,
        pl.BlockSpec((HALO, D_RNN), lambda j: (jnp.maximum(tile_of(j) * n8 - 1, 0), 0)),
        pl.BlockSpec((HALO, D_RNN), lambda j: (jnp.minimum((tile_of(j) + 1) * n8, T_ALL // HALO - 1), 0)),
        pl.BlockSpec((None, CONV_W, D_RNN), lambda j: (l, 0, 0)),
        pl.BlockSpec((None, 1, D_RNN), lambda j: (l, 0, 0)),
        mat, vec(ba), mat, vec(bx), vec(lam),
    ]
    args = [rx, rx, rx, conv_w, conv_b.reshape(DEPTH, 1, D_RNN), wa, ba.reshape(DEPTH, 2, 1, D_RNN),
            wx, bx.reshape(DEPTH, 2, 1, D_RNN), lam.reshape(DEPTH, 2, 1, D_RNN)]
    if final:
        hf, rg, g = extra
        in_specs += [row, row, pl.BlockSpec((1, D_RNN), lambda j: (0, 0))]
        args += [hf, rg, g]
    return pl.pallas_call(
        functools.partial(_rnn_kernel, reverse=reverse, final=final),
        grid=(NT_ALL,),
        in_specs=in_specs,
        out_specs=row,
        out_shape=jax.ShapeDtypeStruct((T_ALL, D_RNN), _F32),
        scratch_shapes=[pltpu.VMEM((8, D_RNN), _F32)],
        compiler_params=_cparams(("arbitrary",)),
        name="rnn_bwd" if reverse else "rnn_fwd",
    )(*args)


def _out_proj_kernel(alat_ref, actx_ref, yr_ref, ag_ref, w_ref, xl_ref, xc_ref, mod_ref, g2_ref, wr_ref, br_ref,
                     xo_ref, h2_ref, ri_ref, rw_ref, cnt_ref, carry_ref):
    i = pl.program_id(0)

    @pl.when(i == 0)
    def _():
        carry_ref[...] = jnp.zeros_like(carry_ref)

    a = jnp.where(i == CTX_TILE, actx_ref[...], alat_ref[...])
    ya = _rms(a, ag_ref[...]).astype(_BF16)
    yr = yr_ref[...].astype(_BF16)
    mix = (jnp.dot(ya, w_ref[0:D_ATTN, :], preferred_element_type=_F32)
           + jnp.dot(yr, w_ref[D_ATTN:D_MODEL, :], preferred_element_type=_F32))
    x = jnp.where(i == CTX_TILE, xc_ref[...], xl_ref[...]) + mod_ref[2:3, :] * mix
    xo_ref[...] = x
    h2 = _rms(x, g2_ref[...]) * (1.0 + mod_ref[4:5, :]) + mod_ref[3:4, :]
    h2_ref[...] = h2

    h2_hi = h2.astype(_BF16)
    h2_lo = (h2 - h2_hi.astype(_F32)).astype(_BF16)
    hh = jnp.dot(h2_hi, wr_ref[...], preferred_element_type=_F32)
    lh = jnp.dot(h2_lo, wr_ref[:, 0:ROUTE_LANES], preferred_element_type=_F32)
    logits = hh[:, 0:ROUTE_LANES] + hh[:, ROUTE_LANES:2 * ROUTE_LANES] + lh + br_ref[...]
    lane = lax.broadcasted_iota(jnp.int32, (TM, ROUTE_LANES), 1)
    lane_f = lane.astype(_F32)
    big = jnp.float32(1 << 20)
    neg = -jnp.inf
    gl = jnp.where(lane < N_GROUPS, logits, neg)
    gmax = jnp.max(gl, axis=-1, keepdims=True)
    gsum = jnp.sum(jnp.exp(gl - gmax), axis=-1, keepdims=True)
    p_g = 1.0 / gsum
    g_idx = jnp.min(jnp.where(gl == gmax, lane_f, big), axis=-1, keepdims=True).astype(jnp.int32)
    lo = E_OFF + EXPERTS_PER_GROUP * g_idx
    el = jnp.where(jnp.logical_and(lane >= lo, lane < lo + EXPERTS_PER_GROUP), logits, neg)
    m1 = jnp.max(el, axis=-1, keepdims=True)
    i1 = jnp.min(jnp.where(el == m1, lane_f, big), axis=-1, keepdims=True).astype(jnp.int32)
    el2 = jnp.where(lane == i1, neg, el)
    m2 = jnp.max(el2, axis=-1, keepdims=True)
    i2 = jnp.min(jnp.where(el2 == m2, lane_f, big), axis=-1, keepdims=True).astype(jnp.int32)
    e21 = jnp.exp(m2 - m1)
    w1 = p_g / (1.0 + e21)
    w2 = p_g * e21 / (1.0 + e21)

    hot1 = lane == i1
    hot2 = lane == i2
    hot = jnp.logical_or(hot1, hot2).astype(_F32)
    rr = lax.broadcasted_iota(jnp.int32, (TM, TM), 0)
    cc = lax.broadcasted_iota(jnp.int32, (TM, TM), 1)
    tri = (cc < rr).astype(_BF16)
    before = jnp.dot(tri, hot.astype(_BF16), preferred_element_type=_F32) + carry_ref[0:1, :]
    rank1 = jnp.sum(jnp.where(hot1, before, 0.0), axis=-1, keepdims=True)
    rank2 = jnp.sum(jnp.where(hot2, before, 0.0), axis=-1, keepdims=True)
    carry = carry_ref[0:1, :] + jnp.sum(hot, axis=0, keepdims=True)
    carry_ref[0:1, :] = carry
    cnt_ref[...] = jnp.broadcast_to(carry, cnt_ref.shape).astype(jnp.int32)

    ri = jnp.where(lane == 0, (i1 - E_OFF).astype(_F32), jnp.where(lane == 1, (i2 - E_OFF).astype(_F32),
                   jnp.where(lane == 2, rank1, jnp.where(lane == 3, rank2, 0.0))))
    ri_ref[...] = ri.T[0:4, :].astype(jnp.int32)
    rw_ref[...] = jnp.where(lane == 0, w1, jnp.where(lane == 1, w2, 0.0))


def _out_proj(n_tiles, alat, actx, yr, ag, w, x_lat, x_ctx, ctx_blk, mod_l, g2, wr, br):
    row = lambda n: pl.BlockSpec((TM, n), lambda i: (i, 0))
    full = lambda a, b: pl.BlockSpec((a, b), lambda i: (0, 0))
    n_rows = n_tiles * TM
    return pl.pallas_call(
        _out_proj_kernel,
        grid=(n_tiles,),
        in_specs=[
            pl.BlockSpec((TM, D_ATTN), lambda i: (jnp.minimum(i, NT_LAT - 1), 0)),
            full(TM, D_ATTN),
            row(D_RNN), full(1, D_ATTN), full(D_MODEL, D_MODEL), *_lat_ctx_specs(ctx_blk),
            pl.BlockSpec((None, 6, D_MODEL), lambda i: (i // NT_LAT, 0, 0)),
            full(1, D_MODEL), full(D_MODEL, 2 * ROUTE_LANES), full(1, ROUTE_LANES),
        ],
        out_specs=[row(D_MODEL), row(D_MODEL), pl.BlockSpec((4, TM), lambda i: (0, i)), row(ROUTE_LANES),
                   full(8, ROUTE_LANES)],
        out_shape=[
            jax.ShapeDtypeStruct((n_rows, D_MODEL), _F32),
            jax.ShapeDtypeStruct((n_rows, D_MODEL), _F32),
            jax.ShapeDtypeStruct((4, n_rows), jnp.int32),
            jax.ShapeDtypeStruct((n_rows, ROUTE_LANES), _F32),
            jax.ShapeDtypeStruct((8, ROUTE_LANES), jnp.int32),
        ],
        scratch_shapes=[pltpu.VMEM((8, ROUTE_LANES), _F32)],
        compiler_params=_cparams(("arbitrary",)),
        name="out_proj",
    )(alat, actx, yr, ag, w, x_lat, x_ctx, mod_l, g2, wr, br)


ROW_DMA_UNROLL = 8


def _row_dmas(n_tok, tile, start_ref, ri_ref, make_copy):
    def copy(t, k):
        tok = tile * TM + t
        slot = start_ref[ri_ref[k * n_tok + tok]] + ri_ref[(2 + k) * n_tok + tok]
        return make_copy(t, k, slot)

    def issue(t, c):
        copy(t, 0).start()
        copy(t, 1).start()
        return c

    def drain(t, c):
        copy(t, 0).wait()
        copy(t, 1).wait()
        return c

    lax.fori_loop(0, TM, issue, 0, unroll=ROW_DMA_UNROLL)
    lax.fori_loop(0, TM, drain, 0, unroll=ROW_DMA_UNROLL)


def _slot_map_kernel(start_ref, ri_ref, tok_ref, *, n_tok, n_slots):
    def init(s, c):
        tok_ref[s] = 0
        return c

    def fill(t, c):
        for k in range(2):
            tok_ref[start_ref[ri_ref[k * n_tok + t]] + ri_ref[(2 + k) * n_tok + t]] = t
        return c

    lax.fori_loop(0, n_slots, init, 0, unroll=ROW_DMA_UNROLL)
    lax.fori_loop(0, n_tok, fill, 0, unroll=ROW_DMA_UNROLL)


def _slot_map(n_tok, n_slots, pad_start, ri_flat):
    return pl.pallas_call(
        functools.partial(_slot_map_kernel, n_tok=n_tok, n_slots=n_slots),
        grid_spec=pltpu.PrefetchScalarGridSpec(
            num_scalar_prefetch=2,
            grid=(1,),
            in_specs=[],
            out_specs=pl.BlockSpec(memory_space=pltpu.SMEM),
        ),
        out_shape=jax.ShapeDtypeStruct((n_slots,), jnp.int32),
        compiler_params=_cparams(("arbitrary",)),
        name="moe_slot_map",
    )(pad_start, ri_flat)


def _expert_kernel(blk_e_ref, n_used_ref, tok_ref, h2_hbm, wgu_ref, wd_ref, y_ref, xbuf, sem):
    del blk_e_ref
    b = pl.program_id(0)
    n_used = n_used_ref[0]

    def gather(blk, wait):
        par = blk % 2

        def body(j, c):
            cp = pltpu.make_async_copy(h2_hbm.at[pl.ds(tok_ref[blk * MOE_BLK + j], 1), :],
                                       xbuf.at[par, pl.ds(j, 1), :], sem.at[par])
            if wait:
                cp.wait()
            else:
                cp.start()
            return c

        lax.fori_loop(0, MOE_BLK, body, 0, unroll=ROW_DMA_UNROLL)

    @pl.when(b == 0)
    def _():
        gather(b, wait=False)

    @pl.when(b + 1 < n_used)
    def _():
        gather(b + 1, wait=False)

    @pl.when(b < n_used)
    def _():
        gather(b, wait=True)
        gu = jnp.dot(xbuf[b % 2], wgu_ref[...], preferred_element_type=_F32)
        g, u = gu[:, 0:D_EXPERT], gu[:, D_EXPERT:2 * D_EXPERT]
        act = g * jax.nn.sigmoid(g) * u
        y_ref[...] = jnp.dot(act, wd_ref[...], preferred_element_type=_F32)

    @pl.when(b >= n_used)
    def _():
        y_ref[...] = jnp.zeros_like(y_ref)


def _experts(l, n_blocks, blk_e, n_used, slot_tok, h2, w_gate_up, w_down):
    return pl.pallas_call(
        _expert_kernel,
        grid_spec=pltpu.PrefetchScalarGridSpec(
            num_scalar_prefetch=3,
            grid=(n_blocks,),
            in_specs=[
                pl.BlockSpec(memory_space=pl.ANY),
                pl.BlockSpec((None, None, D_MODEL, 2 * D_EXPERT), lambda b, be, nu, st: (l, be[b], 0, 0)),
                pl.BlockSpec((None, None, D_EXPERT, D_MODEL), lambda b, be, nu, st: (l, be[b], 0, 0)),
            ],
            out_specs=pl.BlockSpec((MOE_BLK, D_MODEL), lambda b, be, nu, st: (b, 0)),
            scratch_shapes=[pltpu.VMEM((2, MOE_BLK, D_MODEL), _F32), pltpu.SemaphoreType.DMA((2,))],
        ),
        out_shape=jax.ShapeDtypeStruct((n_blocks * MOE_BLK, D_MODEL), _F32),
        compiler_params=_cparams(("arbitrary",)),
        name="moe_experts",
    )(blk_e, n_used, slot_tok, h2, w_gate_up, w_down)


def _combine_kernel(start_ref, ri_ref, ys_hbm, x_ref, rw_ref, mod_ref, o_ref, buf, sem, *, n_tok):
    _row_dmas(n_tok, pl.program_id(0), start_ref, ri_ref,
              lambda t, k, slot: pltpu.make_async_copy(ys_hbm.at[pl.ds(slot, 1), :],
                                                       buf.at[k, pl.ds(t, 1), :], sem))
    rw = rw_ref[...]
    f = buf[0] * rw[:, 0:1] + buf[1] * rw[:, 1:2]
    o_ref[...] = x_ref[...] + mod_ref[5:6, :] * f


def _combine(n_tiles, pad_start, ri_flat, ys, x, rw, mod_l):
    row = lambda n: pl.BlockSpec((TM, n), lambda i, ps, ri: (i, 0))
    return pl.pallas_call(
        functools.partial(_combine_kernel, n_tok=n_tiles * TM),
        grid_spec=pltpu.PrefetchScalarGridSpec(
            num_scalar_prefetch=2,
            grid=(n_tiles,),
            in_specs=[
                pl.BlockSpec(memory_space=pl.ANY),
                row(D_MODEL), row(ROUTE_LANES),
                pl.BlockSpec((None, 6, D_MODEL), lambda i, ps, ri: (i // NT_LAT, 0, 0)),
            ],
            out_specs=row(D_MODEL),
            scratch_shapes=[pltpu.VMEM((2, TM, D_MODEL), _F32), pltpu.SemaphoreType.DMA(())],
        ),
        out_shape=jax.ShapeDtypeStruct((n_tiles * TM, D_MODEL), _F32),
        compiler_params=_cparams(("arbitrary",)),
        name="moe_combine",
    )(pad_start, ri_flat, ys, x, rw, mod_l)


def _moe(l, n_tiles, h2, ri, rw, counts, x, mod_l, w_gate_up, w_down):
    n_tok = n_tiles * TM
    n_blocks = n_tok * 2 // MOE_BLK + N_EXPERTS
    cnt = counts[0, E_OFF:E_OFF + N_EXPERTS]
    padded = (cnt + MOE_BLK - 1) // MOE_BLK * MOE_BLK
    pad_end = jnp.cumsum(padded).astype(jnp.int32)
    pad_start = pad_end - padded
    blk_start = jnp.arange(n_blocks, dtype=jnp.int32) * MOE_BLK
    blk_e = jnp.minimum(jnp.sum((pad_end[None, :] <= blk_start[:, None]).astype(jnp.int32), axis=1),
                        N_EXPERTS - 1).astype(jnp.int32)
    n_used = pad_end[N_EXPERTS - 1:N_EXPERTS] // MOE_BLK
    ri_flat = ri.reshape(-1)
    slot_tok = _slot_map(n_tok, n_blocks * MOE_BLK, pad_start, ri_flat)
    ys = _experts(l, n_blocks, blk_e, n_used, slot_tok, h2, w_gate_up, w_down)
    return _combine(n_tiles, pad_start, ri_flat, ys, x, rw, mod_l)


def _rope_tables():
    pos = jnp.arange(SEQ, dtype=_F32)
    row = jnp.floor(pos / GRID_W)
    col = pos - row * GRID_W
    n_freq = QK_ROPE // 4
    inv = ROPE_THETA ** (-jnp.arange(n_freq, dtype=_F32) / n_freq)
    ang = jnp.concatenate([row[:, None] * inv, col[:, None] * inv], axis=-1)
    cos, sin = jnp.cos(ang), jnp.sin(ang)
    z32 = jnp.zeros((SEQ, 32), _F32)
    z64 = jnp.zeros((SEQ, 64), _F32)
    c = jnp.concatenate([cos, cos, jnp.ones((SEQ, 64), _F32)], axis=-1)
    s1 = jnp.concatenate([-sin, z32, z64], axis=-1)
    s2 = jnp.concatenate([z32, sin, z64], axis=-1)
    ctx_c = jnp.ones((CTX_LEN, 128), _F32)
    ctx_s = jnp.zeros((CTX_LEN, 128), _F32)
    return (jnp.concatenate([c, ctx_c]), jnp.concatenate([s1, ctx_s]), jnp.concatenate([s2, ctx_s]))


def _pad_head_gain(g):
    return jnp.pad(g, ((0, 0), (0, QK_PAD - QK_DIM)))[:, None, :]


def kernel(x, c, ctx, c_ctx, w_mod, b_mod, norm1_g, norm2_g, w_in, q_a_norm_g, kv_a_norm_g, w_uq, w_ukv, q_norm_g, k_norm_g, conv_w, conv_b, lru_wa, lru_ba, lru_wx, lru_bx, lru_lambda, attn_out_norm_g, rnn_out_norm_g, w_out, router_group_w, router_group_b, router_expert_w, router_expert_b, w_gate_up, w_down):
    L = DEPTH
    zc = jnp.zeros((L, D_MODEL, 64), _F32)
    w_in_p = jnp.concatenate([w_in[:, :, 0:832], zc, w_in[:, :, 832:2880]], axis=-1).astype(_BF16)
    w_uq_p = jnp.pad(w_uq.reshape(L, Q_RANK, N_HEADS, QK_DIM), ((0, 0), (0, 0), (0, 0), (0, QK_PAD - QK_DIM)))
    w_uq_p = w_uq_p.reshape(L, Q_RANK, N_HEADS * QK_PAD).astype(_BF16)
    w_ukv_b = w_ukv.astype(_BF16)
    w_out_b = w_out.astype(_BF16)
    qg_p = _pad_head_gain(q_norm_g)
    kgn = k_norm_g[:, None, 0:QK_NOPE]
    kgr = jnp.pad(k_norm_g[:, QK_NOPE:QK_DIM], ((0, 0), (0, 64)))[:, None, :]
    wr = jnp.pad(jnp.concatenate([router_group_w, router_expert_w], axis=-1),
                 ((0, 0), (0, 0), (0, ROUTE_LANES - N_GROUPS - N_EXPERTS)))
    wr_hi = wr.astype(_BF16)
    wr = jnp.concatenate([wr_hi, (wr - wr_hi.astype(_F32)).astype(_BF16)], axis=-1)
    br = jnp.pad(jnp.concatenate([router_group_b, router_expert_b], axis=-1),
                 ((0, 0), (0, ROUTE_LANES - N_GROUPS - N_EXPERTS)))[:, None, :]
    rope_c, rope_s1, rope_s2 = _rope_tables()

    cond_t = jnp.stack([c[0], c_ctx], axis=-1)
    mod = _adaln(cond_t, w_mod, b_mod).reshape(L, 2, 6, D_MODEL)

    x_lat, x_ctx, ctx_blk = x[0], ctx[0], 0
    for l in range(L):
        last = l == L - 1
        n_tiles = NT_LAT if last else NT_ALL
        pq, pkv, rx, rg = _proj_in(x_lat, x_ctx, ctx_blk, mod[l], norm1_g[l][None, :], w_in_p[l])
        q, k, v = _mla_prep(pq, pkv, q_a_norm_g[l][None, :], kv_a_norm_g[l][None, :], w_uq_p[l], w_ukv_b[l],
                            qg_p[l], kgn[l], kgr[l], rope_c, rope_s1, rope_s2)
        attn_lat = _attention(q, k, v, q_tile0=0, n_q=SEQ, tq=TQ, key_tile0=0, n_keys=T_ALL, tk=TM * 3)
        if last:
            attn_ctx = attn_lat[0:TM]
        else:
            attn_ctx = _attention(q, k, v, q_tile0=CTX_TILE, n_q=CTX_LEN, tq=CTX_LEN, key_tile0=CTX_TILE,
                                  n_keys=CTX_LEN, tk=CTX_LEN)
        h_f = _rnn(l, 0, rx, conv_w, conv_b, lru_wa, lru_ba, lru_wx, lru_bx, lru_lambda)
        y_rnn = _rnn(l, 1, rx, conv_w, conv_b, lru_wa, lru_ba, lru_wx, lru_bx, lru_lambda,
                     extra=(h_f, rg, rnn_out_norm_g[l][None, :]))
        x_mid, h2, ri, rw, counts = _out_proj(n_tiles, attn_lat, attn_ctx, y_rnn, attn_out_norm_g[l][None, :],
                                              w_out_b[l], x_lat, x_ctx, ctx_blk, mod[l], norm2_g[l][None, :],
                                              wr[l], br[l])
        x_lat = _moe(l, n_tiles, h2, ri, rw, counts, x_mid, mod[l], w_gate_up, w_down)
        x_ctx, ctx_blk = x_lat, CTX_TILE
    return x_lat.reshape(1, SEQ, D_MODEL)
```

```python
import functools
import math

import jax
import jax.numpy as jnp
from jax import lax
from jax.experimental import pallas as pl
from jax.experimental.pallas import tpu as pltpu

D_MODEL = 2048
SEQ = 8192
CTX_LEN = 256
DEPTH = 4
GRID_W = 64
N_HEADS = 8
D_ATTN = 1024
V_DIM = 128
QK_NOPE = 128
QK_ROPE = 64
QK_DIM = 192
Q_RANK = 512
KV_RANK = 256
D_RNN = 1024
RNN_BLOCKS = 8
RNN_BLOCK = 128
CONV_W = 4
LRU_C = 8.0
N_GROUPS = 4
EXPERTS_PER_GROUP = 8
N_EXPERTS = 32
D_EXPERT = 512
ROPE_THETA = 10000.0
EPS = 1e-6

T_ALL = SEQ + CTX_LEN
TM = 256
NT_ALL = T_ALL // TM
NT_LAT = SEQ // TM
CTX_TILE = NT_LAT
HALO = 8
N_GRP = TM // 8
QK_PAD = 256
V_PAD = 256
KVR_PAD = 384
D_IN_PAD = Q_RANK + KVR_PAD + 2 * D_RNN
ROUTE_LANES = 128
E_OFF = N_GROUPS
MOE_BLK = 256
TQ = 1024
TK = 512
VMEM_LIMIT = 56 * 1024 * 1024

_F32 = jnp.float32
_BF16 = jnp.bfloat16


def _cparams(sem):
    return pltpu.CompilerParams(dimension_semantics=sem, vmem_limit_bytes=VMEM_LIMIT)


def _rms(x, g):
    return x * lax.rsqrt(jnp.mean(x * x, axis=-1, keepdims=True) + EPS) * g


def _adaln_kernel(cond_ref, w_ref, b_ref, o_ref):
    cond = cond_ref[...]
    s = cond * jax.nn.sigmoid(cond)
    w = w_ref[...]
    b = b_ref[...]
    o_ref[0:1, :] = jnp.sum(s[:, 0:1] * w, axis=0, keepdims=True) + b
    o_ref[1:2, :] = jnp.sum(s[:, 1:2] * w, axis=0, keepdims=True) + b


def _adaln(cond_t, w_mod, b_mod):
    tn = 1024
    n_col = 6 * D_MODEL // tn
    return pl.pallas_call(
        _adaln_kernel,
        grid=(DEPTH, n_col),
        in_specs=[
            pl.BlockSpec((D_MODEL, 2), lambda l, j: (0, 0)),
            pl.BlockSpec((None, D_MODEL, tn), lambda l, j: (l, 0, j)),
            pl.BlockSpec((None, 1, tn), lambda l, j: (l, 0, j)),
        ],
        out_specs=pl.BlockSpec((None, 2, tn), lambda l, j: (l, 0, j)),
        out_shape=jax.ShapeDtypeStruct((DEPTH, 2, 6 * D_MODEL), _F32),
        compiler_params=_cparams(("parallel", "parallel")),
        name="adaln",
    )(cond_t, w_mod, b_mod.reshape(DEPTH, 1, 6 * D_MODEL))


def _proj_in_kernel(xl_ref, xc_ref, mod_ref, g_ref, w_ref, pq_ref, pkv_ref, rx_ref, rg_ref):
    x = jnp.where(pl.program_id(0) == CTX_TILE, xc_ref[...], xl_ref[...])
    h = _rms(x, g_ref[...]) * (1.0 + mod_ref[1:2, :]) + mod_ref[0:1, :]
    hb = h.astype(_BF16)
    o0, o1, o2 = Q_RANK, Q_RANK + KVR_PAD, Q_RANK + KVR_PAD + D_RNN
    pq_ref[...] = jnp.dot(hb, w_ref[:, 0:o0], preferred_element_type=_F32)
    pkv_ref[...] = jnp.dot(hb, w_ref[:, o0:o1], preferred_element_type=_F32)
    _split_lanes(rx_ref, jnp.dot(hb, w_ref[:, o1:o2], preferred_element_type=_F32))
    _split_lanes(rg_ref, jnp.dot(hb, w_ref[:, o2:D_IN_PAD], preferred_element_type=_F32))


def _lat_ctx_specs(ctx_blk):
    return [pl.BlockSpec((TM, D_MODEL), lambda i: (jnp.minimum(i, NT_LAT - 1), 0)),
            pl.BlockSpec((TM, D_MODEL), lambda i: (ctx_blk, 0))]


def _proj_in(x_lat, x_ctx, ctx_blk, mod_l, g, w):
    row = lambda n: pl.BlockSpec((TM, n), lambda i: (i, 0))
    return pl.pallas_call(
        _proj_in_kernel,
        grid=(NT_ALL,),
        in_specs=_lat_ctx_specs(ctx_blk) + [
            pl.BlockSpec((None, 6, D_MODEL), lambda i: (i // NT_LAT, 0, 0)),
            pl.BlockSpec((1, D_MODEL), lambda i: (0, 0)),
            pl.BlockSpec((D_MODEL, D_IN_PAD), lambda i: (0, 0)),
        ],
        out_specs=[row(Q_RANK), row(KVR_PAD)] + [pl.BlockSpec((RNN_BLOCKS, TM, RNN_BLOCK), lambda i: (0, i, 0))] * 2,
        out_shape=[
            jax.ShapeDtypeStruct((T_ALL, Q_RANK), _F32),
            jax.ShapeDtypeStruct((T_ALL, KVR_PAD), _F32),
            jax.ShapeDtypeStruct((RNN_BLOCKS, T_ALL, RNN_BLOCK), _F32),
            jax.ShapeDtypeStruct((RNN_BLOCKS, T_ALL, RNN_BLOCK), _F32),
        ],
        compiler_params=_cparams(("parallel",)),
        name="proj_in",
    )(x_lat, x_ctx, mod_l, g, w)


def _rope(y, c, s1, s2):
    return y * c + pltpu.roll(y, 96, axis=1) * s1 + pltpu.roll(y, 32, axis=1) * s2


def _mla_prep_kernel(pq_ref, pkv_ref, qag_ref, kvag_ref, wuq_ref, wukv_ref, qg_ref, kgn_ref, kgr_ref,
                     c_ref, s1_ref, s2_ref, q_ref, k_ref, v_ref):
    c, s1, s2 = c_ref[...], s1_ref[...], s2_ref[...]
    scale = QK_DIM ** -0.5 * math.log2(math.e)
    qa = _rms(pq_ref[...], qag_ref[...]).astype(_BF16)
    qf = jnp.dot(qa, wuq_ref[...], preferred_element_type=_F32)
    pkv = pkv_ref[...]
    kva = _rms(pkv[:, 0:KV_RANK], kvag_ref[...]).astype(_BF16)
    kvf = jnp.dot(kva, wukv_ref[...], preferred_element_type=_F32)
    kr = pkv[:, KV_RANK:KVR_PAD]
    kr_ss = jnp.sum(kr * kr, axis=-1, keepdims=True)
    kr_roped = _rope(kr * kgr_ref[...], c, s1, s2)
    qg = qg_ref[...]
    lane = lax.broadcasted_iota(jnp.int32, (TM, V_PAD - V_DIM), 1)
    ones_col = jnp.where(lane == 0, 1.0, 0.0).astype(_BF16)
    for h in range(N_HEADS):
        qh = qf[:, h * QK_PAD:(h + 1) * QK_PAD]
        r = lax.rsqrt(jnp.sum(qh * qh, axis=-1, keepdims=True) * (1.0 / QK_DIM) + EPS)
        qn = qh * r * qg
        q_ref[h, :, 0:QK_NOPE] = (qn[:, 0:QK_NOPE] * scale).astype(_BF16)
        q_ref[h, :, QK_NOPE:QK_PAD] = (_rope(qn[:, QK_NOPE:QK_PAD], c, s1, s2) * scale).astype(_BF16)
        kn = kvf[:, h * 256:h * 256 + QK_NOPE]
        rk = lax.rsqrt((jnp.sum(kn * kn, axis=-1, keepdims=True) + kr_ss) * (1.0 / QK_DIM) + EPS)
        k_ref[h, :, 0:QK_NOPE] = (kn * rk * kgn_ref[...]).astype(_BF16)
        k_ref[h, :, QK_NOPE:QK_PAD] = (kr_roped * rk).astype(_BF16)
        v_ref[h, :, 0:V_DIM] = kvf[:, h * 256 + QK_NOPE:(h + 1) * 256].astype(_BF16)
        v_ref[h, :, V_DIM:V_PAD] = ones_col


def _mla_prep(pq, pkv, qag, kvag, wuq, wukv, qg, kgn, kgr, rope_c, rope_s1, rope_s2):
    row = lambda n: pl.BlockSpec((TM, n), lambda i: (i, 0))
    full = lambda a, b: pl.BlockSpec((a, b), lambda i: (0, 0))
    head = lambda n: pl.BlockSpec((N_HEADS, TM, n), lambda i: (0, i, 0))
    return pl.pallas_call(
        _mla_prep_kernel,
        grid=(NT_ALL,),
        in_specs=[row(Q_RANK), row(KVR_PAD), full(1, Q_RANK), full(1, KV_RANK),
                  full(Q_RANK, N_HEADS * QK_PAD), full(KV_RANK, N_HEADS * 256),
                  full(1, QK_PAD), full(1, QK_NOPE), full(1, 128),
                  row(128), row(128), row(128)],
        out_specs=[head(QK_PAD), head(QK_PAD), head(V_PAD)],
        out_shape=[jax.ShapeDtypeStruct((N_HEADS, T_ALL, QK_PAD), _BF16),
                   jax.ShapeDtypeStruct((N_HEADS, T_ALL, QK_PAD), _BF16),
                   jax.ShapeDtypeStruct((N_HEADS, T_ALL, V_PAD), _BF16)],
        compiler_params=_cparams(("parallel",)),
        name="mla_prep",
    )(pq, pkv, qag, kvag, wuq, wukv, qg, kgn, kgr, rope_c, rope_s1, rope_s2)


def _attn_kernel(q_ref, k_ref, v_ref, o_ref, sa_ref, sb_ref, acc_ref, *, n_keys, tk):
    q = q_ref[...]
    n_chunks = n_keys // tk
    assert n_chunks % 2 == 1

    def chunk(j):
        return pl.ds(j * tk if isinstance(j, int) else pl.multiple_of(j * tk, tk), tk)

    def scores(j, s_ref, m_cur):
        s = lax.dot_general(q, k_ref[chunk(j), :], (((1,), (1,)), ((), ())), preferred_element_type=_F32)
        s_ref[...] = s
        m = jnp.max(s, axis=-1, keepdims=True)
        return m if m_cur is None else jnp.maximum(m_cur, m)

    def accumulate(j, s_ref, m_prev, m_cur):
        p = jnp.exp2(s_ref[...] - m_cur)
        pv = jnp.dot(p.astype(_BF16), v_ref[chunk(j), :], preferred_element_type=_F32)
        acc_ref[...] = jnp.exp2(m_prev - m_cur) * acc_ref[...] + pv

    def step(i, carry):
        m_prev, m_cur = carry
        j = 2 * i
        m_1 = scores(j + 1, sb_ref, m_cur)
        accumulate(j, sa_ref, m_prev, m_cur)
        m_2 = scores(j + 2, sa_ref, m_1)
        accumulate(j + 1, sb_ref, m_cur, m_1)
        return m_1, m_2

    m0 = scores(0, sa_ref, None)
    acc_ref[...] = jnp.zeros_like(acc_ref)
    m_prev, m_cur = lax.fori_loop(0, (n_chunks - 1) // 2, step, (m0, m0), unroll=True)
    accumulate(n_chunks - 1, sa_ref, m_prev, m_cur)
    acc = acc_ref[...]
    o_ref[...] = acc[:, 0:V_DIM] / acc[:, V_DIM:V_DIM + 1]


def _attention(q, k, v, *, q_tile0, n_q, tq, key_tile0, n_keys, tk):
    return pl.pallas_call(
        functools.partial(_attn_kernel, n_keys=n_keys, tk=tk),
        grid=(N_HEADS, n_q // tq),
        in_specs=[
            pl.BlockSpec((None, tq, QK_PAD), lambda h, i: (h, q_tile0 + i, 0)),
            pl.BlockSpec((None, n_keys, QK_PAD), lambda h, i: (h, key_tile0, 0)),
            pl.BlockSpec((None, n_keys, V_PAD), lambda h, i: (h, key_tile0, 0)),
        ],
        out_specs=pl.BlockSpec((tq, V_DIM), lambda h, i: (i, h)),
        out_shape=jax.ShapeDtypeStruct((n_q, D_ATTN), _F32),
        scratch_shapes=[pltpu.VMEM((tq, tk), _F32), pltpu.VMEM((tq, tk), _F32), pltpu.VMEM((tq, V_PAD), _F32)],
        compiler_params=_cparams(("parallel", "parallel")),
        name="attention",
    )(q, k, v)


def _gelu_tanh(x):
    return 0.5 * x * (1.0 + jnp.tanh(math.sqrt(2.0 / math.pi) * (x + 0.044715 * (x * x * x))))


def _scan_tile(a, b, carry, reverse):
    order = list(range(N_GRP - 1, -1, -1) if reverse else range(N_GRP))
    hs, ps = [None] * N_GRP, [None] * N_GRP
    h = p = None
    for v in order:
        av, bv = a[v * 8:(v + 1) * 8, :], b[v * 8:(v + 1) * 8, :]
        h = bv if h is None else av * h + bv
        p = av if p is None else av * p
        hs[v], ps[v] = h, p
    cin = [None] * 8
    for s in (range(7, -1, -1) if reverse else range(8)):
        cin[s] = carry
        carry = p[s:s + 1, :] * carry + h[s:s + 1, :]
    cin = jnp.concatenate(cin, axis=0)
    return jnp.concatenate([hs[v] + ps[v] * cin for v in range(N_GRP)], axis=0), carry


def _cat_lanes(ref, rows=slice(None)):
    return jnp.concatenate([ref[c, rows, :] for c in range(RNN_BLOCKS)], axis=-1)


def _split_lanes(ref, val):
    for c in range(RNN_BLOCKS):
        ref[c] = val[:, c * RNN_BLOCK:(c + 1) * RNN_BLOCK]


def _load_strided(ref):
    return jnp.concatenate(
        [jnp.concatenate([ref[c, pl.ds(v, 8, stride=N_GRP), :] for v in range(N_GRP)], axis=0)
         for c in range(RNN_BLOCKS)], axis=-1)


def _rnn_kernel(rx_ref, rxp_ref, rxn_ref, cw_ref, cb_ref, wa_ref, ba_ref, wx_ref, bx_ref, lam_ref,
                *rest, reverse, final):
    if final:
        hf_ref, rg_ref, g_ref, o_ref, carry_ref = rest
    else:
        o_ref, carry_ref = rest
    j = pl.program_id(0)
    tile = jnp.where(j == 0, CTX_TILE, (NT_LAT - j) if reverse else (j - 1))

    @pl.when(j == 0)
    def _():
        carry_ref[...] = jnp.zeros_like(carry_ref)

    has_prev = jnp.logical_and(tile >= 1, tile < NT_LAT).astype(_F32)
    has_next = (tile < NT_LAT - 1).astype(_F32)
    p6 = _cat_lanes(rxp_ref, slice(HALO - 2, HALO - 1)) * has_prev
    p7 = _cat_lanes(rxp_ref, slice(HALO - 1, HALO)) * has_prev
    n0 = _cat_lanes(rxn_ref, slice(0, 1)) * has_next
    cur = _load_strided(rx_ref)
    sub = lax.broadcasted_iota(jnp.int32, (8, 1), 0)
    e_m1 = jnp.where(sub == 0, p7, pltpu.roll(cur[TM - 8:TM, :], 1, axis=0))
    e_m2 = jnp.where(sub == 0, p6, pltpu.roll(cur[TM - 16:TM - 8, :], 1, axis=0))
    e_p1 = jnp.where(sub == 7, n0, pltpu.roll(cur[0:8, :], 7, axis=0))
    um1 = jnp.concatenate([e_m1, cur[0:TM - 8, :]], axis=0)
    um2 = jnp.concatenate([e_m2, e_m1, cur[0:TM - 16, :]], axis=0)
    up1 = jnp.concatenate([cur[8:TM, :], e_p1], axis=0)
    u = (cw_ref[0:1, :] * um2 + cw_ref[1:2, :] * um1 + cw_ref[2:3, :] * cur + cw_ref[3:4, :] * up1
         + cb_ref[...])

    ub = u.astype(_BF16)
    r_parts, i_parts = [], []
    for h in range(RNN_BLOCKS):
        uh = ub[:, h * RNN_BLOCK:(h + 1) * RNN_BLOCK]
        r_parts.append(jnp.dot(uh, wa_ref[h], preferred_element_type=_F32))
        i_parts.append(jnp.dot(uh, wx_ref[h], preferred_element_type=_F32))
    r = jax.nn.sigmoid(jnp.concatenate(r_parts, axis=-1) + ba_ref[...])
    i = jax.nn.sigmoid(jnp.concatenate(i_parts, axis=-1) + bx_ref[...])
    nl = -lam_ref[...]
    softplus = jnp.maximum(nl, 0.0) + jnp.log1p(jnp.exp(-jnp.abs(nl)))
    a = jnp.exp((-LRU_C) * r * softplus)
    b = jnp.sqrt(1.0 - a * a) * (i * u)

    h, carry = _scan_tile(a, b, carry_ref[0:1, :], reverse)
    carry_ref[0:1, :] = carry
    if final:
        y = _rms(_gelu_tanh(_load_strided(rg_ref)) * (_cat_lanes(hf_ref) + h), g_ref[...])
        for c in range(RNN_BLOCKS):
            for v in range(N_GRP):
                o_ref[c, pl.ds(v, 8, stride=N_GRP), :] = y[v * 8:(v + 1) * 8, c * RNN_BLOCK:(c + 1) * RNN_BLOCK]
    else:
        _split_lanes(o_ref, h)


def _rnn(l, d, rx, conv_w, conv_b, wa, ba, wx, bx, lam, extra=None):
    reverse = d == 1
    final = extra is not None

    def tile_of(j):
        return jnp.where(j == 0, CTX_TILE, (NT_LAT - j) if reverse else (j - 1))

    n8 = TM // HALO
    row = pl.BlockSpec((RNN_BLOCKS, TM, RNN_BLOCK), lambda j: (0, tile_of(j), 0))
    halo = lambda f: pl.BlockSpec((RNN_BLOCKS, HALO, RNN_BLOCK), lambda j: (0, f(tile_of(j)), 0))
    vec = lambda a: pl.BlockSpec((None, None, 1, D_RNN), lambda j: (l, d, 0, 0))
    mat = pl.BlockSpec((None, None, RNN_BLOCKS, RNN_BLOCK, RNN_BLOCK), lambda j: (l, d, 0, 0, 0))
    in_specs = [
        row,
        halo(lambda t: jnp.maximum(t * n8 - 1, 0)),
        halo(lambda t: jnp.minimum((t + 1) * n8, T_ALL // HALO - 1)),
        pl.BlockSpec((None, CONV_W, D_RNN), lambda j: (l, 0, 0)),
        pl.BlockSpec((None, 1, D_RNN), lambda j: (l, 0, 0)),
        mat, vec(ba), mat, vec(bx), vec(lam),
    ]
    args = [rx, rx, rx, conv_w, conv_b.reshape(DEPTH, 1, D_RNN), wa, ba.reshape(DEPTH, 2, 1, D_RNN),
            wx, bx.reshape(DEPTH, 2, 1, D_RNN), lam.reshape(DEPTH, 2, 1, D_RNN)]
    if final:
        hf, rg, g = extra
        in_specs += [row, row, pl.BlockSpec((1, D_RNN), lambda j: (0, 0))]
        args += [hf, rg, g]
    return pl.pallas_call(
        functools.partial(_rnn_kernel, reverse=reverse, final=final),
        grid=(NT_ALL,),
        in_specs=in_specs,
        out_specs=row,
        out_shape=jax.ShapeDtypeStruct((RNN_BLOCKS, T_ALL, RNN_BLOCK), _F32),
        scratch_shapes=[pltpu.VMEM((8, D_RNN), _F32)],
        compiler_params=_cparams(("arbitrary",)),
        name="rnn_bwd" if reverse else "rnn_fwd",
    )(*args)


def _out_proj_kernel(alat_ref, actx_ref, yr_ref, ag_ref, w_ref, xl_ref, xc_ref, mod_ref, g2_ref, wr_ref, br_ref,
                     xo_ref, h2_ref, ri_ref, rw_ref, cnt_ref, carry_ref):
    i = pl.program_id(0)

    @pl.when(i == 0)
    def _():
        carry_ref[...] = jnp.zeros_like(carry_ref)

    a = jnp.where(i == CTX_TILE, actx_ref[...], alat_ref[...])
    ya = _rms(a, ag_ref[...]).astype(_BF16)
    yr = _cat_lanes(yr_ref).astype(_BF16)
    mix = (jnp.dot(ya, w_ref[0:D_ATTN, :], preferred_element_type=_F32)
           + jnp.dot(yr, w_ref[D_ATTN:D_MODEL, :], preferred_element_type=_F32))
    x = jnp.where(i == CTX_TILE, xc_ref[...], xl_ref[...]) + mod_ref[2:3, :] * mix
    xo_ref[...] = x
    h2 = _rms(x, g2_ref[...]) * (1.0 + mod_ref[4:5, :]) + mod_ref[3:4, :]
    h2_ref[...] = h2

    h2_hi = h2.astype(_BF16)
    h2_lo = (h2 - h2_hi.astype(_F32)).astype(_BF16)
    hh = jnp.dot(h2_hi, wr_ref[...], preferred_element_type=_F32)
    lh = jnp.dot(h2_lo, wr_ref[:, 0:ROUTE_LANES], preferred_element_type=_F32)
    logits = hh[:, 0:ROUTE_LANES] + hh[:, ROUTE_LANES:2 * ROUTE_LANES] + lh + br_ref[...]
    lane = lax.broadcasted_iota(jnp.int32, (TM, ROUTE_LANES), 1)
    lane_f = lane.astype(_F32)
    big = jnp.float32(1 << 20)
    neg = -jnp.inf
    gl = jnp.where(lane < N_GROUPS, logits, neg)
    gmax = jnp.max(gl, axis=-1, keepdims=True)
    gsum = jnp.sum(jnp.exp(gl - gmax), axis=-1, keepdims=True)
    p_g = 1.0 / gsum
    g_idx = jnp.min(jnp.where(gl == gmax, lane_f, big), axis=-1, keepdims=True).astype(jnp.int32)
    lo = E_OFF + EXPERTS_PER_GROUP * g_idx
    el = jnp.where(jnp.logical_and(lane >= lo, lane < lo + EXPERTS_PER_GROUP), logits, neg)
    m1 = jnp.max(el, axis=-1, keepdims=True)
    i1 = jnp.min(jnp.where(el == m1, lane_f, big), axis=-1, keepdims=True).astype(jnp.int32)
    el2 = jnp.where(lane == i1, neg, el)
    m2 = jnp.max(el2, axis=-1, keepdims=True)
    i2 = jnp.min(jnp.where(el2 == m2, lane_f, big), axis=-1, keepdims=True).astype(jnp.int32)
    e21 = jnp.exp(m2 - m1)
    w1 = p_g / (1.0 + e21)
    w2 = p_g * e21 / (1.0 + e21)

    hot1 = lane == i1
    hot2 = lane == i2
    hot = jnp.logical_or(hot1, hot2).astype(_F32)
    rr = lax.broadcasted_iota(jnp.int32, (TM, TM), 0)
    cc = lax.broadcasted_iota(jnp.int32, (TM, TM), 1)
    tri = (cc < rr).astype(_BF16)
    before = jnp.dot(tri, hot.astype(_BF16), preferred_element_type=_F32) + carry_ref[0:1, :]
    rank1 = jnp.sum(jnp.where(hot1, before, 0.0), axis=-1, keepdims=True)
    rank2 = jnp.sum(jnp.where(hot2, before, 0.0), axis=-1, keepdims=True)
    carry = carry_ref[0:1, :] + jnp.sum(hot, axis=0, keepdims=True)
    carry_ref[0:1, :] = carry
    cnt_ref[...] = jnp.broadcast_to(carry, cnt_ref.shape).astype(jnp.int32)

    ri = jnp.where(lane == 0, (i1 - E_OFF).astype(_F32), jnp.where(lane == 1, (i2 - E_OFF).astype(_F32),
                   jnp.where(lane == 2, rank1, jnp.where(lane == 3, rank2, 0.0))))
    ri_ref[...] = ri.T[0:4, :].astype(jnp.int32)
    rw_ref[...] = jnp.where(lane == 0, w1, jnp.where(lane == 1, w2, 0.0))


def _out_proj(n_tiles, alat, actx, yr, ag, w, x_lat, x_ctx, ctx_blk, mod_l, g2, wr, br):
    row = lambda n: pl.BlockSpec((TM, n), lambda i: (i, 0))
    full = lambda a, b: pl.BlockSpec((a, b), lambda i: (0, 0))
    n_rows = n_tiles * TM
    return pl.pallas_call(
        _out_proj_kernel,
        grid=(n_tiles,),
        in_specs=[
            pl.BlockSpec((TM, D_ATTN), lambda i: (jnp.minimum(i, NT_LAT - 1), 0)),
            full(TM, D_ATTN),
            pl.BlockSpec((RNN_BLOCKS, TM, RNN_BLOCK), lambda i: (0, i, 0)),
            full(1, D_ATTN), full(D_MODEL, D_MODEL), *_lat_ctx_specs(ctx_blk),
            pl.BlockSpec((None, 6, D_MODEL), lambda i: (i // NT_LAT, 0, 0)),
            full(1, D_MODEL), full(D_MODEL, 2 * ROUTE_LANES), full(1, ROUTE_LANES),
        ],
        out_specs=[row(D_MODEL), row(D_MODEL), pl.BlockSpec((4, TM), lambda i: (0, i)), row(ROUTE_LANES),
                   full(8, ROUTE_LANES)],
        out_shape=[
            jax.ShapeDtypeStruct((n_rows, D_MODEL), _F32),
            jax.ShapeDtypeStruct((n_rows, D_MODEL), _F32),
            jax.ShapeDtypeStruct((4, n_rows), jnp.int32),
            jax.ShapeDtypeStruct((n_rows, ROUTE_LANES), _F32),
            jax.ShapeDtypeStruct((8, ROUTE_LANES), jnp.int32),
        ],
        scratch_shapes=[pltpu.VMEM((8, ROUTE_LANES), _F32)],
        compiler_params=_cparams(("arbitrary",)),
        name="out_proj",
    )(alat, actx, yr, ag, w, x_lat, x_ctx, mod_l, g2, wr, br)


ROW_DMA_UNROLL = 8


def _row_dmas(n_tok, tile, start_ref, ri_ref, make_copy):
    def copy(t, k):
        tok = tile * TM + t
        slot = start_ref[ri_ref[k * n_tok + tok]] + ri_ref[(2 + k) * n_tok + tok]
        return make_copy(t, k, slot)

    def issue(t, c):
        copy(t, 0).start()
        copy(t, 1).start()
        return c

    def drain(t, c):
        copy(t, 0).wait()
        copy(t, 1).wait()
        return c

    lax.fori_loop(0, TM, issue, 0, unroll=ROW_DMA_UNROLL)
    lax.fori_loop(0, TM, drain, 0, unroll=ROW_DMA_UNROLL)


def _dispatch_kernel(start_ref, ri_ref, h2_ref, xs_in_hbm, xs_hbm, sem, *, n_tok):
    del xs_in_hbm
    _row_dmas(n_tok, pl.program_id(0), start_ref, ri_ref,
              lambda t, k, slot: pltpu.make_async_copy(h2_ref.at[pl.ds(t, 1), :],
                                                       xs_hbm.at[pl.ds(slot, 1), :], sem))


def _dispatch(n_tiles, n_slots, pad_start, ri_flat, h2):
    xs0 = jnp.zeros((n_slots, D_MODEL), _F32)
    return pl.pallas_call(
        functools.partial(_dispatch_kernel, n_tok=n_tiles * TM),
        grid_spec=pltpu.PrefetchScalarGridSpec(
            num_scalar_prefetch=2,
            grid=(n_tiles,),
            in_specs=[pl.BlockSpec((TM, D_MODEL), lambda i, ps, ri: (i, 0)), pl.BlockSpec(memory_space=pl.ANY)],
            out_specs=pl.BlockSpec(memory_space=pl.ANY),
            scratch_shapes=[pltpu.SemaphoreType.DMA(())],
        ),
        out_shape=jax.ShapeDtypeStruct((n_slots, D_MODEL), _F32),
        input_output_aliases={3: 0},
        compiler_params=_cparams(("arbitrary",)),
        name="moe_dispatch",
    )(pad_start, ri_flat, h2, xs0)


def _expert_kernel(blk_e_ref, n_used_ref, x_ref, wgu_ref, wd_ref, y_ref):
    del blk_e_ref
    used = pl.program_id(0) < n_used_ref[0]

    @pl.when(used)
    def _():
        gu = jnp.dot(x_ref[...], wgu_ref[...], preferred_element_type=_F32)
        g, u = gu[:, 0:D_EXPERT], gu[:, D_EXPERT:2 * D_EXPERT]
        act = g * jax.nn.sigmoid(g) * u
        y_ref[...] = jnp.dot(act, wd_ref[...], preferred_element_type=_F32)

    @pl.when(jnp.logical_not(used))
    def _():
        y_ref[...] = jnp.zeros_like(y_ref)


def _experts(l, n_blocks, blk_e, n_used, xs, w_gate_up, w_down):
    blk = lambda b, be, nu: (jnp.minimum(b, nu[0] - 1), 0)
    return pl.pallas_call(
        _expert_kernel,
        grid_spec=pltpu.PrefetchScalarGridSpec(
            num_scalar_prefetch=2,
            grid=(n_blocks,),
            in_specs=[
                pl.BlockSpec((MOE_BLK, D_MODEL), blk),
                pl.BlockSpec((None, None, D_MODEL, 2 * D_EXPERT), lambda b, be, nu: (l, be[b], 0, 0)),
                pl.BlockSpec((None, None, D_EXPERT, D_MODEL), lambda b, be, nu: (l, be[b], 0, 0)),
            ],
            out_specs=pl.BlockSpec((MOE_BLK, D_MODEL), lambda b, be, nu: (b, 0)),
        ),
        out_shape=jax.ShapeDtypeStruct((n_blocks * MOE_BLK, D_MODEL), _F32),
        compiler_params=_cparams(("arbitrary",)),
        name="moe_experts",
    )(blk_e, n_used, xs, w_gate_up, w_down)


def _combine_kernel(start_ref, ri_ref, ys_hbm, x_ref, rw_ref, mod_ref, o_ref, buf, sem, *, n_tok):
    _row_dmas(n_tok, pl.program_id(0), start_ref, ri_ref,
              lambda t, k, slot: pltpu.make_async_copy(ys_hbm.at[pl.ds(slot, 1), :],
                                                       buf.at[k, pl.ds(t, 1), :], sem))
    rw = rw_ref[...]
    f = buf[0] * rw[:, 0:1] + buf[1] * rw[:, 1:2]
    o_ref[...] = x_ref[...] + mod_ref[5:6, :] * f


def _combine(n_tiles, pad_start, ri_flat, ys, x, rw, mod_l):
    row = lambda n: pl.BlockSpec((TM, n), lambda i, ps, ri: (i, 0))
    return pl.pallas_call(
        functools.partial(_combine_kernel, n_tok=n_tiles * TM),
        grid_spec=pltpu.PrefetchScalarGridSpec(
            num_scalar_prefetch=2,
            grid=(n_tiles,),
            in_specs=[
                pl.BlockSpec(memory_space=pl.ANY),
                row(D_MODEL), row(ROUTE_LANES),
                pl.BlockSpec((None, 6, D_MODEL), lambda i, ps, ri: (i // NT_LAT, 0, 0)),
            ],
            out_specs=row(D_MODEL),
            scratch_shapes=[pltpu.VMEM((2, TM, D_MODEL), _F32), pltpu.SemaphoreType.DMA(())],
        ),
        out_shape=jax.ShapeDtypeStruct((n_tiles * TM, D_MODEL), _F32),
        compiler_params=_cparams(("arbitrary",)),
        name="moe_combine",
    )(pad_start, ri_flat, ys, x, rw, mod_l)


def _moe(l, n_tiles, h2, ri, rw, counts, x, mod_l, w_gate_up, w_down):
    n_tok = n_tiles * TM
    n_blocks = n_tok * 2 // MOE_BLK + N_EXPERTS
    cnt = counts[0, E_OFF:E_OFF + N_EXPERTS]
    padded = (cnt + MOE_BLK - 1) // MOE_BLK * MOE_BLK
    pad_end = jnp.cumsum(padded).astype(jnp.int32)
    pad_start = pad_end - padded
    blk_start = jnp.arange(n_blocks, dtype=jnp.int32) * MOE_BLK
    blk_e = jnp.minimum(jnp.sum((pad_end[None, :] <= blk_start[:, None]).astype(jnp.int32), axis=1),
                        N_EXPERTS - 1).astype(jnp.int32)
    n_used = pad_end[N_EXPERTS - 1:N_EXPERTS] // MOE_BLK
    ri_flat = ri.reshape(-1)
    xs = _dispatch(n_tiles, n_blocks * MOE_BLK, pad_start, ri_flat, h2)
    ys = _experts(l, n_blocks, blk_e, n_used, xs, w_gate_up, w_down)
    return _combine(n_tiles, pad_start, ri_flat, ys, x, rw, mod_l)


def _rope_tables():
    pos = jnp.arange(SEQ, dtype=_F32)
    row = jnp.floor(pos / GRID_W)
    col = pos - row * GRID_W
    n_freq = QK_ROPE // 4
    inv = ROPE_THETA ** (-jnp.arange(n_freq, dtype=_F32) / n_freq)
    ang = jnp.concatenate([row[:, None] * inv, col[:, None] * inv], axis=-1)
    cos, sin = jnp.cos(ang), jnp.sin(ang)
    z32 = jnp.zeros((SEQ, 32), _F32)
    z64 = jnp.zeros((SEQ, 64), _F32)
    c = jnp.concatenate([cos, cos, jnp.ones((SEQ, 64), _F32)], axis=-1)
    s1 = jnp.concatenate([-sin, z32, z64], axis=-1)
    s2 = jnp.concatenate([z32, sin, z64], axis=-1)
    ctx_c = jnp.ones((CTX_LEN, 128), _F32)
    ctx_s = jnp.zeros((CTX_LEN, 128), _F32)
    return (jnp.concatenate([c, ctx_c]), jnp.concatenate([s1, ctx_s]), jnp.concatenate([s2, ctx_s]))


def _pad_head_gain(g):
    return jnp.pad(g, ((0, 0), (0, QK_PAD - QK_DIM)))[:, None, :]


def kernel(x, c, ctx, c_ctx, w_mod, b_mod, norm1_g, norm2_g, w_in, q_a_norm_g, kv_a_norm_g, w_uq, w_ukv, q_norm_g, k_norm_g, conv_w, conv_b, lru_wa, lru_ba, lru_wx, lru_bx, lru_lambda, attn_out_norm_g, rnn_out_norm_g, w_out, router_group_w, router_group_b, router_expert_w, router_expert_b, w_gate_up, w_down):
    L = DEPTH
    zc = jnp.zeros((L, D_MODEL, 64), _F32)
    w_in_p = jnp.concatenate([w_in[:, :, 0:832], zc, w_in[:, :, 832:2880]], axis=-1).astype(_BF16)
    w_uq_p = jnp.pad(w_uq.reshape(L, Q_RANK, N_HEADS, QK_DIM), ((0, 0), (0, 0), (0, 0), (0, QK_PAD - QK_DIM)))
    w_uq_p = w_uq_p.reshape(L, Q_RANK, N_HEADS * QK_PAD).astype(_BF16)
    w_ukv_b = w_ukv.astype(_BF16)
    w_out_b = w_out.astype(_BF16)
    qg_p = _pad_head_gain(q_norm_g)
    kgn = k_norm_g[:, None, 0:QK_NOPE]
    kgr = jnp.pad(k_norm_g[:, QK_NOPE:QK_DIM], ((0, 0), (0, 64)))[:, None, :]
    wr = jnp.pad(jnp.concatenate([router_group_w, router_expert_w], axis=-1),
                 ((0, 0), (0, 0), (0, ROUTE_LANES - N_GROUPS - N_EXPERTS)))
    wr_hi = wr.astype(_BF16)
    wr = jnp.concatenate([wr_hi, (wr - wr_hi.astype(_F32)).astype(_BF16)], axis=-1)
    br = jnp.pad(jnp.concatenate([router_group_b, router_expert_b], axis=-1),
                 ((0, 0), (0, ROUTE_LANES - N_GROUPS - N_EXPERTS)))[:, None, :]
    rope_c, rope_s1, rope_s2 = _rope_tables()

    cond_t = jnp.stack([c[0], c_ctx], axis=-1)
    mod = _adaln(cond_t, w_mod, b_mod).reshape(L, 2, 6, D_MODEL)

    x_lat, x_ctx, ctx_blk = x[0], ctx[0], 0
    for l in range(L):
        last = l == L - 1
        n_tiles = NT_LAT if last else NT_ALL
        pq, pkv, rx, rg = _proj_in(x_lat, x_ctx, ctx_blk, mod[l], norm1_g[l][None, :], w_in_p[l])
        q, k, v = _mla_prep(pq, pkv, q_a_norm_g[l][None, :], kv_a_norm_g[l][None, :], w_uq_p[l], w_ukv_b[l],
                            qg_p[l], kgn[l], kgr[l], rope_c, rope_s1, rope_s2)
        attn_lat = _attention(q, k, v, q_tile0=0, n_q=SEQ, tq=TQ, key_tile0=0, n_keys=T_ALL, tk=TM * 11)
        if last:
            attn_ctx = attn_lat[0:TM]
        else:
            attn_ctx = _attention(q, k, v, q_tile0=CTX_TILE, n_q=CTX_LEN, tq=CTX_LEN, key_tile0=CTX_TILE,
                                  n_keys=CTX_LEN, tk=CTX_LEN)
        h_f = _rnn(l, 0, rx, conv_w, conv_b, lru_wa, lru_ba, lru_wx, lru_bx, lru_lambda)
        y_rnn = _rnn(l, 1, rx, conv_w, conv_b, lru_wa, lru_ba, lru_wx, lru_bx, lru_lambda,
                     extra=(h_f, rg, rnn_out_norm_g[l][None, :]))
        x_mid, h2, ri, rw, counts = _out_proj(n_tiles, attn_lat, attn_ctx, y_rnn, attn_out_norm_g[l][None, :],
                                              w_out_b[l], x_lat, x_ctx, ctx_blk, mod[l], norm2_g[l][None, :],
                                              wr[l], br[l])
        x_lat = _moe(l, n_tiles, h2, ri, rw, counts, x_mid, mod[l], w_gate_up, w_down)
        x_ctx, ctx_blk = x_lat, CTX_TILE
    return x_lat.reshape(1, SEQ, D_MODEL)
```

```python
import functools
import math

import jax
import jax.numpy as jnp
from jax import lax
from jax.experimental import pallas as pl
from jax.experimental.pallas import tpu as pltpu

D_MODEL = 2048
SEQ = 8192
CTX_LEN = 256
DEPTH = 4
GRID_W = 64
N_HEADS = 8
D_ATTN = 1024
V_DIM = 128
QK_NOPE = 128
QK_ROPE = 64
QK_DIM = 192
Q_RANK = 512
KV_RANK = 256
D_RNN = 1024
RNN_BLOCKS = 8
RNN_BLOCK = 128
CONV_W = 4
LRU_C = 8.0
N_GROUPS = 4
EXPERTS_PER_GROUP = 8
N_EXPERTS = 32
D_EXPERT = 512
ROPE_THETA = 10000.0
EPS = 1e-6

T_ALL = SEQ + CTX_LEN
TM = 256
NT_ALL = T_ALL // TM
NT_LAT = SEQ // TM
CTX_TILE = NT_LAT
HALO = 8
N_GRP = TM // 8
QK_PAD = 256
V_PAD = 256
KVR_PAD = 384
D_IN_PAD = Q_RANK + KVR_PAD + 2 * D_RNN
ROUTE_LANES = 128
E_OFF = N_GROUPS
MOE_BLK = 256
TQ = 1024
TK = 512
VMEM_LIMIT = 56 * 1024 * 1024

_F32 = jnp.float32
_BF16 = jnp.bfloat16


def _cparams(sem):
    return pltpu.CompilerParams(dimension_semantics=sem, vmem_limit_bytes=VMEM_LIMIT)


def _rms(x, g):
    return x * lax.rsqrt(jnp.mean(x * x, axis=-1, keepdims=True) + EPS) * g


def _adaln_kernel(cond_ref, w_ref, b_ref, o_ref):
    cond = cond_ref[...]
    s = cond * jax.nn.sigmoid(cond)
    w = w_ref[...]
    b = b_ref[...]
    o_ref[0:1, :] = jnp.sum(s[:, 0:1] * w, axis=0, keepdims=True) + b
    o_ref[1:2, :] = jnp.sum(s[:, 1:2] * w, axis=0, keepdims=True) + b


def _adaln(cond_t, w_mod, b_mod):
    tn = 1024
    n_col = 6 * D_MODEL // tn
    return pl.pallas_call(
        _adaln_kernel,
        grid=(DEPTH, n_col),
        in_specs=[
            pl.BlockSpec((D_MODEL, 2), lambda l, j: (0, 0)),
            pl.BlockSpec((None, D_MODEL, tn), lambda l, j: (l, 0, j)),
            pl.BlockSpec((None, 1, tn), lambda l, j: (l, 0, j)),
        ],
        out_specs=pl.BlockSpec((None, 2, tn), lambda l, j: (l, 0, j)),
        out_shape=jax.ShapeDtypeStruct((DEPTH, 2, 6 * D_MODEL), _F32),
        compiler_params=_cparams(("parallel", "parallel")),
        name="adaln",
    )(cond_t, w_mod, b_mod.reshape(DEPTH, 1, 6 * D_MODEL))


def _proj_in_kernel(xl_ref, xc_ref, mod_ref, g_ref, w_ref, pq_ref, pkv_ref, rx_ref, rg_ref):
    x = jnp.where(pl.program_id(0) == CTX_TILE, xc_ref[...], xl_ref[...])
    h = _rms(x, g_ref[...]) * (1.0 + mod_ref[1:2, :]) + mod_ref[0:1, :]
    hb = h.astype(_BF16)
    o0, o1, o2 = Q_RANK, Q_RANK + KVR_PAD, Q_RANK + KVR_PAD + D_RNN
    pq_ref[...] = jnp.dot(hb, w_ref[:, 0:o0], preferred_element_type=_F32)
    pkv_ref[...] = jnp.dot(hb, w_ref[:, o0:o1], preferred_element_type=_F32)
    hs = jnp.dot(_time_perm(to_strided=True), hb, preferred_element_type=_F32).astype(_BF16)
    _split_lanes(rx_ref, jnp.dot(hs, w_ref[:, o1:o2], preferred_element_type=_F32))
    _split_lanes(rg_ref, jnp.dot(hs, w_ref[:, o2:D_IN_PAD], preferred_element_type=_F32))


def _lat_ctx_specs(ctx_blk):
    return [pl.BlockSpec((TM, D_MODEL), lambda i: (jnp.minimum(i, NT_LAT - 1), 0)),
            pl.BlockSpec((TM, D_MODEL), lambda i: (ctx_blk, 0))]


def _proj_in(x_lat, x_ctx, ctx_blk, mod_l, g, w):
    row = lambda n: pl.BlockSpec((TM, n), lambda i: (i, 0))
    return pl.pallas_call(
        _proj_in_kernel,
        grid=(NT_ALL,),
        in_specs=_lat_ctx_specs(ctx_blk) + [
            pl.BlockSpec((None, 6, D_MODEL), lambda i: (i // NT_LAT, 0, 0)),
            pl.BlockSpec((1, D_MODEL), lambda i: (0, 0)),
            pl.BlockSpec((D_MODEL, D_IN_PAD), lambda i: (0, 0)),
        ],
        out_specs=[row(Q_RANK), row(KVR_PAD)] + [pl.BlockSpec((RNN_BLOCKS, TM, RNN_BLOCK), lambda i: (0, i, 0))] * 2,
        out_shape=[
            jax.ShapeDtypeStruct((T_ALL, Q_RANK), _F32),
            jax.ShapeDtypeStruct((T_ALL, KVR_PAD), _F32),
            jax.ShapeDtypeStruct((RNN_BLOCKS, T_ALL, RNN_BLOCK), _F32),
            jax.ShapeDtypeStruct((RNN_BLOCKS, T_ALL, RNN_BLOCK), _F32),
        ],
        compiler_params=_cparams(("parallel",)),
        name="proj_in",
    )(x_lat, x_ctx, mod_l, g, w)


def _rope(y, c, s1, s2):
    return y * c + pltpu.roll(y, 96, axis=1) * s1 + pltpu.roll(y, 32, axis=1) * s2


def _mla_prep_kernel(pq_ref, pkv_ref, qag_ref, kvag_ref, wuq_ref, wukv_ref, qg_ref, kgn_ref, kgr_ref,
                     c_ref, s1_ref, s2_ref, q_ref, k_ref, v_ref):
    c, s1, s2 = c_ref[...], s1_ref[...], s2_ref[...]
    scale = QK_DIM ** -0.5 * math.log2(math.e)
    qa = _rms(pq_ref[...], qag_ref[...]).astype(_BF16)
    qf = jnp.dot(qa, wuq_ref[...], preferred_element_type=_F32)
    pkv = pkv_ref[...]
    kva = _rms(pkv[:, 0:KV_RANK], kvag_ref[...]).astype(_BF16)
    kvf = jnp.dot(kva, wukv_ref[...], preferred_element_type=_F32)
    kr = pkv[:, KV_RANK:KVR_PAD]
    kr_ss = jnp.sum(kr * kr, axis=-1, keepdims=True)
    kr_roped = _rope(kr * kgr_ref[...], c, s1, s2)
    qg = qg_ref[...]
    lane = lax.broadcasted_iota(jnp.int32, (TM, V_PAD - V_DIM), 1)
    ones_col = jnp.where(lane == 0, 1.0, 0.0).astype(_BF16)
    for h in range(N_HEADS):
        qh = qf[:, h * QK_PAD:(h + 1) * QK_PAD]
        r = lax.rsqrt(jnp.sum(qh * qh, axis=-1, keepdims=True) * (1.0 / QK_DIM) + EPS)
        qn = qh * r * qg
        q_ref[h, :, 0:QK_NOPE] = (qn[:, 0:QK_NOPE] * scale).astype(_BF16)
        q_ref[h, :, QK_NOPE:QK_PAD] = (_rope(qn[:, QK_NOPE:QK_PAD], c, s1, s2) * scale).astype(_BF16)
        kn = kvf[:, h * 256:h * 256 + QK_NOPE]
        rk = lax.rsqrt((jnp.sum(kn * kn, axis=-1, keepdims=True) + kr_ss) * (1.0 / QK_DIM) + EPS)
        k_ref[h, :, 0:QK_NOPE] = (kn * rk * kgn_ref[...]).astype(_BF16)
        k_ref[h, :, QK_NOPE:QK_PAD] = (kr_roped * rk).astype(_BF16)
        v_ref[h, :, 0:V_DIM] = kvf[:, h * 256 + QK_NOPE:(h + 1) * 256].astype(_BF16)
        v_ref[h, :, V_DIM:V_PAD] = ones_col


def _mla_prep(pq, pkv, qag, kvag, wuq, wukv, qg, kgn, kgr, rope_c, rope_s1, rope_s2):
    row = lambda n: pl.BlockSpec((TM, n), lambda i: (i, 0))
    full = lambda a, b: pl.BlockSpec((a, b), lambda i: (0, 0))
    head = lambda n: pl.BlockSpec((N_HEADS, TM, n), lambda i: (0, i, 0))
    return pl.pallas_call(
        _mla_prep_kernel,
        grid=(NT_ALL,),
        in_specs=[row(Q_RANK), row(KVR_PAD), full(1, Q_RANK), full(1, KV_RANK),
                  full(Q_RANK, N_HEADS * QK_PAD), full(KV_RANK, N_HEADS * 256),
                  full(1, QK_PAD), full(1, QK_NOPE), full(1, 128),
                  row(128), row(128), row(128)],
        out_specs=[head(QK_PAD), head(QK_PAD), head(V_PAD)],
        out_shape=[jax.ShapeDtypeStruct((N_HEADS, T_ALL, QK_PAD), _BF16),
                   jax.ShapeDtypeStruct((N_HEADS, T_ALL, QK_PAD), _BF16),
                   jax.ShapeDtypeStruct((N_HEADS, T_ALL, V_PAD), _BF16)],
        compiler_params=_cparams(("parallel",)),
        name="mla_prep",
    )(pq, pkv, qag, kvag, wuq, wukv, qg, kgn, kgr, rope_c, rope_s1, rope_s2)


def _attn_kernel(q_ref, k_ref, v_ref, o_ref, sa_ref, sb_ref, acc_ref, *, n_keys, tk):
    q = q_ref[...]
    n_chunks = n_keys // tk
    assert n_chunks % 2 == 1

    def chunk(j):
        return pl.ds(j * tk if isinstance(j, int) else pl.multiple_of(j * tk, tk), tk)

    def scores(j, s_ref, m_cur):
        s = lax.dot_general(q, k_ref[chunk(j), :], (((1,), (1,)), ((), ())), preferred_element_type=_F32)
        s_ref[...] = s
        m = jnp.max(s, axis=-1, keepdims=True)
        return m if m_cur is None else jnp.maximum(m_cur, m)

    def accumulate(j, s_ref, m_prev, m_cur):
        p = jnp.exp2(s_ref[...] - m_cur)
        pv = jnp.dot(p.astype(_BF16), v_ref[chunk(j), :], preferred_element_type=_F32)
        acc_ref[...] = jnp.exp2(m_prev - m_cur) * acc_ref[...] + pv

    def step(i, carry):
        m_prev, m_cur = carry
        j = 2 * i
        m_1 = scores(j + 1, sb_ref, m_cur)
        accumulate(j, sa_ref, m_prev, m_cur)
        m_2 = scores(j + 2, sa_ref, m_1)
        accumulate(j + 1, sb_ref, m_cur, m_1)
        return m_1, m_2

    m0 = scores(0, sa_ref, None)
    acc_ref[...] = jnp.zeros_like(acc_ref)
    m_prev, m_cur = lax.fori_loop(0, (n_chunks - 1) // 2, step, (m0, m0), unroll=True)
    accumulate(n_chunks - 1, sa_ref, m_prev, m_cur)
    acc = acc_ref[...]
    o_ref[...] = acc[:, 0:V_DIM] / acc[:, V_DIM:V_DIM + 1]


def _attention(q, k, v, *, q_tile0, n_q, tq, key_tile0, n_keys, tk):
    return pl.pallas_call(
        functools.partial(_attn_kernel, n_keys=n_keys, tk=tk),
        grid=(N_HEADS, n_q // tq),
        in_specs=[
            pl.BlockSpec((None, tq, QK_PAD), lambda h, i: (h, q_tile0 + i, 0)),
            pl.BlockSpec((None, n_keys, QK_PAD), lambda h, i: (h, key_tile0, 0)),
            pl.BlockSpec((None, n_keys, V_PAD), lambda h, i: (h, key_tile0, 0)),
        ],
        out_specs=pl.BlockSpec((tq, V_DIM), lambda h, i: (i, h)),
        out_shape=jax.ShapeDtypeStruct((n_q, D_ATTN), _F32),
        scratch_shapes=[pltpu.VMEM((tq, tk), _F32), pltpu.VMEM((tq, tk), _F32), pltpu.VMEM((tq, V_PAD), _F32)],
        compiler_params=_cparams(("parallel", "parallel")),
        name="attention",
    )(q, k, v)


def _gelu_tanh(x):
    return 0.5 * x * (1.0 + jnp.tanh(math.sqrt(2.0 / math.pi) * (x + 0.044715 * (x * x * x))))


def _scan_tile(a, b, carry, reverse):
    order = list(range(N_GRP - 1, -1, -1) if reverse else range(N_GRP))
    hs, ps = [None] * N_GRP, [None] * N_GRP
    h = p = None
    for v in order:
        av, bv = a[v * 8:(v + 1) * 8, :], b[v * 8:(v + 1) * 8, :]
        h = bv if h is None else av * h + bv
        p = av if p is None else av * p
        hs[v], ps[v] = h, p
    cin = [None] * 8
    for s in (range(7, -1, -1) if reverse else range(8)):
        cin[s] = carry
        carry = p[s:s + 1, :] * carry + h[s:s + 1, :]
    cin = jnp.concatenate(cin, axis=0)
    return jnp.concatenate([hs[v] + ps[v] * cin for v in range(N_GRP)], axis=0), carry


def _cat_lanes(ref, rows=slice(None)):
    return jnp.concatenate([ref[c, rows, :] for c in range(RNN_BLOCKS)], axis=-1)


def _split_lanes(ref, val):
    for c in range(RNN_BLOCKS):
        ref[c] = val[:, c * RNN_BLOCK:(c + 1) * RNN_BLOCK]


def _time_perm(to_strided):
    r = lax.broadcasted_iota(jnp.int32, (TM, TM), 0)
    c = lax.broadcasted_iota(jnp.int32, (TM, TM), 1)
    if to_strided:
        src = (r % 8) * N_GRP + r // 8
    else:
        src = (r % N_GRP) * 8 + r // N_GRP
    return jnp.where(c == src, 1.0, 0.0).astype(_BF16)


def _rnn_kernel(rx_ref, rxp_ref, rxn_ref, cw_ref, cb_ref, wa_ref, ba_ref, wx_ref, bx_ref, lam_ref,
                *rest, reverse, final):
    if final:
        hf_ref, rg_ref, g_ref, o_ref, carry_ref = rest
    else:
        o_ref, carry_ref = rest
    j = pl.program_id(0)
    tile = jnp.where(j == 0, CTX_TILE, (NT_LAT - j) if reverse else (j - 1))

    @pl.when(j == 0)
    def _():
        carry_ref[...] = jnp.zeros_like(carry_ref)

    has_prev = jnp.logical_and(tile >= 1, tile < NT_LAT).astype(_F32)
    has_next = (tile < NT_LAT - 1).astype(_F32)
    p6 = _cat_lanes(rxp_ref, slice(7, 8)) * has_prev
    p7 = _cat_lanes(rxp_ref, slice(15, 16)) * has_prev
    n0 = _cat_lanes(rxn_ref, slice(0, 1)) * has_next
    cur = _cat_lanes(rx_ref)
    sub = lax.broadcasted_iota(jnp.int32, (8, 1), 0)
    e_m1 = jnp.where(sub == 0, p7, pltpu.roll(cur[TM - 8:TM, :], 1, axis=0))
    e_m2 = jnp.where(sub == 0, p6, pltpu.roll(cur[TM - 16:TM - 8, :], 1, axis=0))
    e_p1 = jnp.where(sub == 7, n0, pltpu.roll(cur[0:8, :], 7, axis=0))
    um1 = jnp.concatenate([e_m1, cur[0:TM - 8, :]], axis=0)
    um2 = jnp.concatenate([e_m2, e_m1, cur[0:TM - 16, :]], axis=0)
    up1 = jnp.concatenate([cur[8:TM, :], e_p1], axis=0)
    u = (cw_ref[0:1, :] * um2 + cw_ref[1:2, :] * um1 + cw_ref[2:3, :] * cur + cw_ref[3:4, :] * up1
         + cb_ref[...])

    ub = u.astype(_BF16)
    r_parts, i_parts = [], []
    for h in range(RNN_BLOCKS):
        uh = ub[:, h * RNN_BLOCK:(h + 1) * RNN_BLOCK]
        r_parts.append(jnp.dot(uh, wa_ref[h], preferred_element_type=_F32))
        i_parts.append(jnp.dot(uh, wx_ref[h], preferred_element_type=_F32))
    r = jax.nn.sigmoid(jnp.concatenate(r_parts, axis=-1) + ba_ref[...])
    i = jax.nn.sigmoid(jnp.concatenate(i_parts, axis=-1) + bx_ref[...])
    nl = -lam_ref[...]
    softplus = jnp.maximum(nl, 0.0) + jnp.log1p(jnp.exp(-jnp.abs(nl)))
    a = jnp.exp((-LRU_C) * r * softplus)
    b = jnp.sqrt(1.0 - a * a) * (i * u)

    h, carry = _scan_tile(a, b, carry_ref[0:1, :], reverse)
    carry_ref[0:1, :] = carry
    if final:
        y = _rms(_gelu_tanh(_cat_lanes(rg_ref)) * (_cat_lanes(hf_ref) + h), g_ref[...]).astype(_BF16)
        o_ref[...] = jnp.dot(_time_perm(to_strided=False), y, preferred_element_type=_F32).astype(_BF16)
    else:
        _split_lanes(o_ref, h)


def _rnn(l, d, rx, conv_w, conv_b, wa, ba, wx, bx, lam, extra=None):
    reverse = d == 1
    final = extra is not None

    def tile_of(j):
        return jnp.where(j == 0, CTX_TILE, (NT_LAT - j) if reverse else (j - 1))

    row = pl.BlockSpec((RNN_BLOCKS, TM, RNN_BLOCK), lambda j: (0, tile_of(j), 0))
    halo = lambda n, f: pl.BlockSpec((RNN_BLOCKS, n, RNN_BLOCK), lambda j: (0, f(tile_of(j)), 0))
    vec = lambda a: pl.BlockSpec((None, None, 1, D_RNN), lambda j: (l, d, 0, 0))
    mat = pl.BlockSpec((None, None, RNN_BLOCKS, RNN_BLOCK, RNN_BLOCK), lambda j: (l, d, 0, 0, 0))
    in_specs = [
        row,
        halo(2 * HALO, lambda t: jnp.maximum(t * (TM // (2 * HALO)) - 1, 0)),
        halo(HALO, lambda t: jnp.minimum((t + 1) * (TM // HALO), T_ALL // HALO - 1)),
        pl.BlockSpec((None, CONV_W, D_RNN), lambda j: (l, 0, 0)),
        pl.BlockSpec((None, 1, D_RNN), lambda j: (l, 0, 0)),
        mat, vec(ba), mat, vec(bx), vec(lam),
    ]
    args = [rx, rx, rx, conv_w, conv_b.reshape(DEPTH, 1, D_RNN), wa, ba.reshape(DEPTH, 2, 1, D_RNN),
            wx, bx.reshape(DEPTH, 2, 1, D_RNN), lam.reshape(DEPTH, 2, 1, D_RNN)]
    if final:
        hf, rg, g = extra
        in_specs += [row, row, pl.BlockSpec((1, D_RNN), lambda j: (0, 0))]
        args += [hf, rg, g]
    return pl.pallas_call(
        functools.partial(_rnn_kernel, reverse=reverse, final=final),
        grid=(NT_ALL,),
        in_specs=in_specs,
        out_specs=pl.BlockSpec((TM, D_RNN), lambda j: (tile_of(j), 0)) if final else row,
        out_shape=(jax.ShapeDtypeStruct((T_ALL, D_RNN), _BF16) if final
                   else jax.ShapeDtypeStruct((RNN_BLOCKS, T_ALL, RNN_BLOCK), _F32)),
        scratch_shapes=[pltpu.VMEM((8, D_RNN), _F32)],
        compiler_params=_cparams(("arbitrary",)),
        name="rnn_bwd" if reverse else "rnn_fwd",
    )(*args)


def _out_proj_kernel(alat_ref, actx_ref, yr_ref, ag_ref, w_ref, xl_ref, xc_ref, mod_ref, g2_ref, wr_ref, br_ref,
                     xo_ref, h2_ref, ri_ref, rw_ref, cnt_ref, carry_ref):
    i = pl.program_id(0)

    @pl.when(i == 0)
    def _():
        carry_ref[...] = jnp.zeros_like(carry_ref)

    a = jnp.where(i == CTX_TILE, actx_ref[...], alat_ref[...])
    ya = _rms(a, ag_ref[...]).astype(_BF16)
    yr = yr_ref[...]
    mix = (jnp.dot(ya, w_ref[0:D_ATTN, :], preferred_element_type=_F32)
           + jnp.dot(yr, w_ref[D_ATTN:D_MODEL, :], preferred_element_type=_F32))
    x = jnp.where(i == CTX_TILE, xc_ref[...], xl_ref[...]) + mod_ref[2:3, :] * mix
    xo_ref[...] = x
    h2 = _rms(x, g2_ref[...]) * (1.0 + mod_ref[4:5, :]) + mod_ref[3:4, :]
    h2_ref[...] = h2

    h2_hi = h2.astype(_BF16)
    h2_lo = (h2 - h2_hi.astype(_F32)).astype(_BF16)
    hh = jnp.dot(h2_hi, wr_ref[...], preferred_element_type=_F32)
    lh = jnp.dot(h2_lo, wr_ref[:, 0:ROUTE_LANES], preferred_element_type=_F32)
    logits = hh[:, 0:ROUTE_LANES] + hh[:, ROUTE_LANES:2 * ROUTE_LANES] + lh + br_ref[...]
    lane = lax.broadcasted_iota(jnp.int32, (TM, ROUTE_LANES), 1)
    lane_f = lane.astype(_F32)
    big = jnp.float32(1 << 20)
    neg = -jnp.inf
    gl = jnp.where(lane < N_GROUPS, logits, neg)
    gmax = jnp.max(gl, axis=-1, keepdims=True)
    gsum = jnp.sum(jnp.exp(gl - gmax), axis=-1, keepdims=True)
    p_g = 1.0 / gsum
    g_idx = jnp.min(jnp.where(gl == gmax, lane_f, big), axis=-1, keepdims=True).astype(jnp.int32)
    lo = E_OFF + EXPERTS_PER_GROUP * g_idx
    el = jnp.where(jnp.logical_and(lane >= lo, lane < lo + EXPERTS_PER_GROUP), logits, neg)
    m1 = jnp.max(el, axis=-1, keepdims=True)
    i1 = jnp.min(jnp.where(el == m1, lane_f, big), axis=-1, keepdims=True).astype(jnp.int32)
    el2 = jnp.where(lane == i1, neg, el)
    m2 = jnp.max(el2, axis=-1, keepdims=True)
    i2 = jnp.min(jnp.where(el2 == m2, lane_f, big), axis=-1, keepdims=True).astype(jnp.int32)
    e21 = jnp.exp(m2 - m1)
    w1 = p_g / (1.0 + e21)
    w2 = p_g * e21 / (1.0 + e21)

    hot1 = lane == i1
    hot2 = lane == i2
    hot = jnp.logical_or(hot1, hot2).astype(_F32)
    rr = lax.broadcasted_iota(jnp.int32, (TM, TM), 0)
    cc = lax.broadcasted_iota(jnp.int32, (TM, TM), 1)
    tri = (cc < rr).astype(_BF16)
    before = jnp.dot(tri, hot.astype(_BF16), preferred_element_type=_F32) + carry_ref[0:1, :]
    rank1 = jnp.sum(jnp.where(hot1, before, 0.0), axis=-1, keepdims=True)
    rank2 = jnp.sum(jnp.where(hot2, before, 0.0), axis=-1, keepdims=True)
    carry = carry_ref[0:1, :] + jnp.sum(hot, axis=0, keepdims=True)
    carry_ref[0:1, :] = carry
    cnt_ref[...] = jnp.broadcast_to(carry, cnt_ref.shape).astype(jnp.int32)

    ri = jnp.where(lane == 0, (i1 - E_OFF).astype(_F32), jnp.where(lane == 1, (i2 - E_OFF).astype(_F32),
                   jnp.where(lane == 2, rank1, jnp.where(lane == 3, rank2, 0.0))))
    ri_ref[...] = ri.T[0:4, :].astype(jnp.int32)
    rw_ref[...] = jnp.where(lane == 0, w1, jnp.where(lane == 1, w2, 0.0))


def _out_proj(n_tiles, alat, actx, yr, ag, w, x_lat, x_ctx, ctx_blk, mod_l, g2, wr, br):
    row = lambda n: pl.BlockSpec((TM, n), lambda i: (i, 0))
    full = lambda a, b: pl.BlockSpec((a, b), lambda i: (0, 0))
    n_rows = n_tiles * TM
    return pl.pallas_call(
        _out_proj_kernel,
        grid=(n_tiles,),
        in_specs=[
            pl.BlockSpec((TM, D_ATTN), lambda i: (jnp.minimum(i, NT_LAT - 1), 0)),
            full(TM, D_ATTN),
            row(D_RNN), full(1, D_ATTN), full(D_MODEL, D_MODEL), *_lat_ctx_specs(ctx_blk),
            pl.BlockSpec((None, 6, D_MODEL), lambda i: (i // NT_LAT, 0, 0)),
            full(1, D_MODEL), full(D_MODEL, 2 * ROUTE_LANES), full(1, ROUTE_LANES),
        ],
        out_specs=[row(D_MODEL), row(D_MODEL), pl.BlockSpec((4, TM), lambda i: (0, i)), row(ROUTE_LANES),
                   full(8, ROUTE_LANES)],
        out_shape=[
            jax.ShapeDtypeStruct((n_rows, D_MODEL), _F32),
            jax.ShapeDtypeStruct((n_rows, D_MODEL), _F32),
            jax.ShapeDtypeStruct((4, n_rows), jnp.int32),
            jax.ShapeDtypeStruct((n_rows, ROUTE_LANES), _F32),
            jax.ShapeDtypeStruct((8, ROUTE_LANES), jnp.int32),
        ],
        scratch_shapes=[pltpu.VMEM((8, ROUTE_LANES), _F32)],
        compiler_params=_cparams(("arbitrary",)),
        name="out_proj",
    )(alat, actx, yr, ag, w, x_lat, x_ctx, mod_l, g2, wr, br)


ROW_DMA_UNROLL = 8


def _row_dmas(n_tok, tile, start_ref, ri_ref, make_copy):
    def copy(t, k):
        tok = tile * TM + t
        slot = start_ref[ri_ref[k * n_tok + tok]] + ri_ref[(2 + k) * n_tok + tok]
        return make_copy(t, k, slot)

    def issue(t, c):
        copy(t, 0).start()
        copy(t, 1).start()
        return c

    def drain(t, c):
        copy(t, 0).wait()
        copy(t, 1).wait()
        return c

    lax.fori_loop(0, TM, issue, 0, unroll=ROW_DMA_UNROLL)
    lax.fori_loop(0, TM, drain, 0, unroll=ROW_DMA_UNROLL)


def _dispatch_kernel(start_ref, ri_ref, h2_ref, xs_in_hbm, xs_hbm, sem, *, n_tok):
    del xs_in_hbm
    _row_dmas(n_tok, pl.program_id(0), start_ref, ri_ref,
              lambda t, k, slot: pltpu.make_async_copy(h2_ref.at[pl.ds(t, 1), :],
                                                       xs_hbm.at[pl.ds(slot, 1), :], sem))


def _dispatch(n_tiles, n_slots, pad_start, ri_flat, h2):
    xs0 = jnp.zeros((n_slots, D_MODEL), _F32)
    return pl.pallas_call(
        functools.partial(_dispatch_kernel, n_tok=n_tiles * TM),
        grid_spec=pltpu.PrefetchScalarGridSpec(
            num_scalar_prefetch=2,
            grid=(n_tiles,),
            in_specs=[pl.BlockSpec((TM, D_MODEL), lambda i, ps, ri: (i, 0)), pl.BlockSpec(memory_space=pl.ANY)],
            out_specs=pl.BlockSpec(memory_space=pl.ANY),
            scratch_shapes=[pltpu.SemaphoreType.DMA(())],
        ),
        out_shape=jax.ShapeDtypeStruct((n_slots, D_MODEL), _F32),
        input_output_aliases={3: 0},
        compiler_params=_cparams(("arbitrary",)),
        name="moe_dispatch",
    )(pad_start, ri_flat, h2, xs0)


def _expert_kernel(blk_e_ref, n_used_ref, x_ref, wgu_ref, wd_ref, y_ref):
    del blk_e_ref
    used = pl.program_id(0) < n_used_ref[0]

    @pl.when(used)
    def _():
        gu = jnp.dot(x_ref[...], wgu_ref[...], preferred_element_type=_F32)
        g, u = gu[:, 0:D_EXPERT], gu[:, D_EXPERT:2 * D_EXPERT]
        act = g * jax.nn.sigmoid(g) * u
        y_ref[...] = jnp.dot(act, wd_ref[...], preferred_element_type=_F32)

    @pl.when(jnp.logical_not(used))
    def _():
        y_ref[...] = jnp.zeros_like(y_ref)


def _experts(l, n_blocks, blk_e, n_used, xs, w_gate_up, w_down):
    blk = lambda b, be, nu: (jnp.minimum(b, nu[0] - 1), 0)
    return pl.pallas_call(
        _expert_kernel,
        grid_spec=pltpu.PrefetchScalarGridSpec(
            num_scalar_prefetch=2,
            grid=(n_blocks,),
            in_specs=[
                pl.BlockSpec((MOE_BLK, D_MODEL), blk),
                pl.BlockSpec((None, None, D_MODEL, 2 * D_EXPERT), lambda b, be, nu: (l, be[b], 0, 0)),
                pl.BlockSpec((None, None, D_EXPERT, D_MODEL), lambda b, be, nu: (l, be[b], 0, 0)),
            ],
            out_specs=pl.BlockSpec((MOE_BLK, D_MODEL), lambda b, be, nu: (b, 0)),
        ),
        out_shape=jax.ShapeDtypeStruct((n_blocks * MOE_BLK, D_MODEL), _F32),
        compiler_params=_cparams(("arbitrary",)),
        name="moe_experts",
    )(blk_e, n_used, xs, w_gate_up, w_down)


def _combine_kernel(start_ref, ri_ref, ys_hbm, x_ref, rw_ref, mod_ref, o_ref, buf, sem, *, n_tok):
    _row_dmas(n_tok, pl.program_id(0), start_ref, ri_ref,
              lambda t, k, slot: pltpu.make_async_copy(ys_hbm.at[pl.ds(slot, 1), :],
                                                       buf.at[k, pl.ds(t, 1), :], sem))
    rw = rw_ref[...]
    f = buf[0] * rw[:, 0:1] + buf[1] * rw[:, 1:2]
    o_ref[...] = x_ref[...] + mod_ref[5:6, :] * f


def _combine(n_tiles, pad_start, ri_flat, ys, x, rw, mod_l):
    row = lambda n: pl.BlockSpec((TM, n), lambda i, ps, ri: (i, 0))
    return pl.pallas_call(
        functools.partial(_combine_kernel, n_tok=n_tiles * TM),
        grid_spec=pltpu.PrefetchScalarGridSpec(
            num_scalar_prefetch=2,
            grid=(n_tiles,),
            in_specs=[
                pl.BlockSpec(memory_space=pl.ANY),
                row(D_MODEL), row(ROUTE_LANES),
                pl.BlockSpec((None, 6, D_MODEL), lambda i, ps, ri: (i // NT_LAT, 0, 0)),
            ],
            out_specs=row(D_MODEL),
            scratch_shapes=[pltpu.VMEM((2, TM, D_MODEL), _F32), pltpu.SemaphoreType.DMA(())],
        ),
        out_shape=jax.ShapeDtypeStruct((n_tiles * TM, D_MODEL), _F32),
        compiler_params=_cparams(("arbitrary",)),
        name="moe_combine",
    )(pad_start, ri_flat, ys, x, rw, mod_l)


def _moe(l, n_tiles, h2, ri, rw, counts, x, mod_l, w_gate_up, w_down):
    n_tok = n_tiles * TM
    n_blocks = n_tok * 2 // MOE_BLK + N_EXPERTS
    cnt = counts[0, E_OFF:E_OFF + N_EXPERTS]
    padded = (cnt + MOE_BLK - 1) // MOE_BLK * MOE_BLK
    pad_end = jnp.cumsum(padded).astype(jnp.int32)
    pad_start = pad_end - padded
    blk_start = jnp.arange(n_blocks, dtype=jnp.int32) * MOE_BLK
    blk_e = jnp.minimum(jnp.sum((pad_end[None, :] <= blk_start[:, None]).astype(jnp.int32), axis=1),
                        N_EXPERTS - 1).astype(jnp.int32)
    n_used = pad_end[N_EXPERTS - 1:N_EXPERTS] // MOE_BLK
    ri_flat = ri.reshape(-1)
    xs = _dispatch(n_tiles, n_blocks * MOE_BLK, pad_start, ri_flat, h2)
    ys = _experts(l, n_blocks, blk_e, n_used, xs, w_gate_up, w_down)
    return _combine(n_tiles, pad_start, ri_flat, ys, x, rw, mod_l)


def _rope_tables():
    pos = jnp.arange(SEQ, dtype=_F32)
    row = jnp.floor(pos / GRID_W)
    col = pos - row * GRID_W
    n_freq = QK_ROPE // 4
    inv = ROPE_THETA ** (-jnp.arange(n_freq, dtype=_F32) / n_freq)
    ang = jnp.concatenate([row[:, None] * inv, col[:, None] * inv], axis=-1)
    cos, sin = jnp.cos(ang), jnp.sin(ang)
    z32 = jnp.zeros((SEQ, 32), _F32)
    z64 = jnp.zeros((SEQ, 64), _F32)
    c = jnp.concatenate([cos, cos, jnp.ones((SEQ, 64), _F32)], axis=-1)
    s1 = jnp.concatenate([-sin, z32, z64], axis=-1)
    s2 = jnp.concatenate([z32, sin, z64], axis=-1)
    ctx_c = jnp.ones((CTX_LEN, 128), _F32)
    ctx_s = jnp.zeros((CTX_LEN, 128), _F32)
    return (jnp.concatenate([c, ctx_c]), jnp.concatenate([s1, ctx_s]), jnp.concatenate([s2, ctx_s]))


def _pad_head_gain(g):
    return jnp.pad(g, ((0, 0), (0, QK_PAD - QK_DIM)))[:, None, :]


def kernel(x, c, ctx, c_ctx, w_mod, b_mod, norm1_g, norm2_g, w_in, q_a_norm_g, kv_a_norm_g, w_uq, w_ukv, q_norm_g, k_norm_g, conv_w, conv_b, lru_wa, lru_ba, lru_wx, lru_bx, lru_lambda, attn_out_norm_g, rnn_out_norm_g, w_out, router_group_w, router_group_b, router_expert_w, router_expert_b, w_gate_up, w_down):
    L = DEPTH
    zc = jnp.zeros((L, D_MODEL, 64), _F32)
    w_in_p = jnp.concatenate([w_in[:, :, 0:832], zc, w_in[:, :, 832:2880]], axis=-1).astype(_BF16)
    w_uq_p = jnp.pad(w_uq.reshape(L, Q_RANK, N_HEADS, QK_DIM), ((0, 0), (0, 0), (0, 0), (0, QK_PAD - QK_DIM)))
    w_uq_p = w_uq_p.reshape(L, Q_RANK, N_HEADS * QK_PAD).astype(_BF16)
    w_ukv_b = w_ukv.astype(_BF16)
    w_out_b = w_out.astype(_BF16)
    qg_p = _pad_head_gain(q_norm_g)
    kgn = k_norm_g[:, None, 0:QK_NOPE]
    kgr = jnp.pad(k_norm_g[:, QK_NOPE:QK_DIM], ((0, 0), (0, 64)))[:, None, :]
    wr = jnp.pad(jnp.concatenate([router_group_w, router_expert_w], axis=-1),
                 ((0, 0), (0, 0), (0, ROUTE_LANES - N_GROUPS - N_EXPERTS)))
    wr_hi = wr.astype(_BF16)
    wr = jnp.concatenate([wr_hi, (wr - wr_hi.astype(_F32)).astype(_BF16)], axis=-1)
    br = jnp.pad(jnp.concatenate([router_group_b, router_expert_b], axis=-1),
                 ((0, 0), (0, ROUTE_LANES - N_GROUPS - N_EXPERTS)))[:, None, :]
    rope_c, rope_s1, rope_s2 = _rope_tables()

    cond_t = jnp.stack([c[0], c_ctx], axis=-1)
    mod = _adaln(cond_t, w_mod, b_mod).reshape(L, 2, 6, D_MODEL)

    x_lat, x_ctx, ctx_blk = x[0], ctx[0], 0
    for l in range(L):
        last = l == L - 1
        n_tiles = NT_LAT if last else NT_ALL
        pq, pkv, rx, rg = _proj_in(x_lat, x_ctx, ctx_blk, mod[l], norm1_g[l][None, :], w_in_p[l])
        q, k, v = _mla_prep(pq, pkv, q_a_norm_g[l][None, :], kv_a_norm_g[l][None, :], w_uq_p[l], w_ukv_b[l],
                            qg_p[l], kgn[l], kgr[l], rope_c, rope_s1, rope_s2)
        attn_lat = _attention(q, k, v, q_tile0=0, n_q=SEQ, tq=TQ, key_tile0=0, n_keys=T_ALL, tk=TM * 11)
        if last:
            attn_ctx = attn_lat[0:TM]
        else:
            attn_ctx = _attention(q, k, v, q_tile0=CTX_TILE, n_q=CTX_LEN, tq=CTX_LEN, key_tile0=CTX_TILE,
                                  n_keys=CTX_LEN, tk=CTX_LEN)
        h_f = _rnn(l, 0, rx, conv_w, conv_b, lru_wa, lru_ba, lru_wx, lru_bx, lru_lambda)
        y_rnn = _rnn(l, 1, rx, conv_w, conv_b, lru_wa, lru_ba, lru_wx, lru_bx, lru_lambda,
                     extra=(h_f, rg, rnn_out_norm_g[l][None, :]))
        x_mid, h2, ri, rw, counts = _out_proj(n_tiles, attn_lat, attn_ctx, y_rnn, attn_out_norm_g[l][None, :],
                                              w_out_b[l], x_lat, x_ctx, ctx_blk, mod[l], norm2_g[l][None, :],
                                              wr[l], br[l])
        x_lat = _moe(l, n_tiles, h2, ri, rw, counts, x_mid, mod[l], w_gate_up, w_down)
        x_ctx, ctx_blk = x_lat, CTX_TILE
    return x_lat.reshape(1, SEQ, D_MODEL)
```

```python
import functools
import math

import jax
import jax.numpy as jnp
from jax import lax
from jax.experimental import pallas as pl
from jax.experimental.pallas import tpu as pltpu

D_MODEL = 2048
SEQ = 8192
CTX_LEN = 256
DEPTH = 4
GRID_W = 64
N_HEADS = 8
D_ATTN = 1024
V_DIM = 128
QK_NOPE = 128
QK_ROPE = 64
QK_DIM = 192
Q_RANK = 512
KV_RANK = 256
D_RNN = 1024
RNN_BLOCKS = 8
RNN_BLOCK = 128
CONV_W = 4
LRU_C = 8.0
N_GROUPS = 4
EXPERTS_PER_GROUP = 8
N_EXPERTS = 32
D_EXPERT = 512
ROPE_THETA = 10000.0
EPS = 1e-6

T_ALL = SEQ + CTX_LEN
TM = 256
NT_ALL = T_ALL // TM
NT_LAT = SEQ // TM
CTX_TILE = NT_LAT
HALO = 8
N_GRP = TM // 8
QK_PAD = 256
V_PAD = 256
KVR_PAD = 384
D_IN_PAD = Q_RANK + KVR_PAD + 2 * D_RNN
ROUTE_LANES = 128
E_OFF = N_GROUPS
MOE_BLK = 256
TQ = 1024
TK = 512
VMEM_LIMIT = 56 * 1024 * 1024

_F32 = jnp.float32
_BF16 = jnp.bfloat16


def _cparams(sem):
    return pltpu.CompilerParams(dimension_semantics=sem, vmem_limit_bytes=VMEM_LIMIT)


def _rms(x, g):
    return x * lax.rsqrt(jnp.mean(x * x, axis=-1, keepdims=True) + EPS) * g


def _adaln_kernel(cond_ref, w_ref, b_ref, o_ref):
    cond = cond_ref[...]
    s = cond * jax.nn.sigmoid(cond)
    w = w_ref[...]
    b = b_ref[...]
    o_ref[0:1, :] = jnp.sum(s[:, 0:1] * w, axis=0, keepdims=True) + b
    o_ref[1:2, :] = jnp.sum(s[:, 1:2] * w, axis=0, keepdims=True) + b


def _adaln(cond_t, w_mod, b_mod):
    tn = 1024
    n_col = 6 * D_MODEL // tn
    return pl.pallas_call(
        _adaln_kernel,
        grid=(DEPTH, n_col),
        in_specs=[
            pl.BlockSpec((D_MODEL, 2), lambda l, j: (0, 0)),
            pl.BlockSpec((None, D_MODEL, tn), lambda l, j: (l, 0, j)),
            pl.BlockSpec((None, 1, tn), lambda l, j: (l, 0, j)),
        ],
        out_specs=pl.BlockSpec((None, 2, tn), lambda l, j: (l, 0, j)),
        out_shape=jax.ShapeDtypeStruct((DEPTH, 2, 6 * D_MODEL), _F32),
        compiler_params=_cparams(("parallel", "parallel")),
        name="adaln",
    )(cond_t, w_mod, b_mod.reshape(DEPTH, 1, 6 * D_MODEL))


def _proj_in_kernel(xl_ref, xc_ref, mod_ref, g_ref, w_ref, pq_ref, pkv_ref, rx_ref, rg_ref):
    x = jnp.where(pl.program_id(0) == CTX_TILE, xc_ref[...], xl_ref[...])
    h = _rms(x, g_ref[...]) * (1.0 + mod_ref[1:2, :]) + mod_ref[0:1, :]
    hb = h.astype(_BF16)
    o0, o1, o2 = Q_RANK, Q_RANK + KVR_PAD, Q_RANK + KVR_PAD + D_RNN
    pq_ref[...] = jnp.dot(hb, w_ref[:, 0:o0], preferred_element_type=_F32)
    pkv_ref[...] = jnp.dot(hb, w_ref[:, o0:o1], preferred_element_type=_F32)
    hs = jnp.dot(_time_perm(to_strided=True), hb, preferred_element_type=_F32).astype(_BF16)
    _split_lanes(rx_ref, jnp.dot(hs, w_ref[:, o1:o2], preferred_element_type=_F32))
    _split_lanes(rg_ref, jnp.dot(hs, w_ref[:, o2:D_IN_PAD], preferred_element_type=_F32))


def _lat_ctx_specs(ctx_blk):
    return [pl.BlockSpec((TM, D_MODEL), lambda i: (jnp.minimum(i, NT_LAT - 1), 0)),
            pl.BlockSpec((TM, D_MODEL), lambda i: (ctx_blk, 0))]


def _proj_in(x_lat, x_ctx, ctx_blk, mod_l, g, w):
    row = lambda n: pl.BlockSpec((TM, n), lambda i: (i, 0))
    return pl.pallas_call(
        _proj_in_kernel,
        grid=(NT_ALL,),
        in_specs=_lat_ctx_specs(ctx_blk) + [
            pl.BlockSpec((None, 6, D_MODEL), lambda i: (i // NT_LAT, 0, 0)),
            pl.BlockSpec((1, D_MODEL), lambda i: (0, 0)),
            pl.BlockSpec((D_MODEL, D_IN_PAD), lambda i: (0, 0)),
        ],
        out_specs=[row(Q_RANK), row(KVR_PAD)] + [pl.BlockSpec((RNN_BLOCKS, TM, RNN_BLOCK), lambda i: (0, i, 0))] * 2,
        out_shape=[
            jax.ShapeDtypeStruct((T_ALL, Q_RANK), _F32),
            jax.ShapeDtypeStruct((T_ALL, KVR_PAD), _F32),
            jax.ShapeDtypeStruct((RNN_BLOCKS, T_ALL, RNN_BLOCK), _F32),
            jax.ShapeDtypeStruct((RNN_BLOCKS, T_ALL, RNN_BLOCK), _F32),
        ],
        compiler_params=_cparams(("parallel",)),
        name="proj_in",
    )(x_lat, x_ctx, mod_l, g, w)


def _rope(y, c, s1, s2):
    return y * c + pltpu.roll(y, 96, axis=1) * s1 + pltpu.roll(y, 32, axis=1) * s2


def _mla_prep_kernel(pq_ref, pkv_ref, qag_ref, kvag_ref, wuq_ref, wukv_ref, qg_ref, kgn_ref, kgr_ref,
                     c_ref, s1_ref, s2_ref, q_ref, k_ref, v_ref):
    c, s1, s2 = c_ref[...], s1_ref[...], s2_ref[...]
    scale = QK_DIM ** -0.5 * math.log2(math.e)
    qa = _rms(pq_ref[...], qag_ref[...]).astype(_BF16)
    qf = jnp.dot(qa, wuq_ref[...], preferred_element_type=_F32)
    pkv = pkv_ref[...]
    kva = _rms(pkv[:, 0:KV_RANK], kvag_ref[...]).astype(_BF16)
    kvf = jnp.dot(kva, wukv_ref[...], preferred_element_type=_F32)
    kr = pkv[:, KV_RANK:KVR_PAD]
    kr_ss = jnp.sum(kr * kr, axis=-1, keepdims=True)
    kr_roped = _rope(kr * kgr_ref[...], c, s1, s2)
    qg = qg_ref[...]
    lane = lax.broadcasted_iota(jnp.int32, (TM, V_PAD - V_DIM), 1)
    ones_col = jnp.where(lane == 0, 1.0, 0.0).astype(_BF16)
    for h in range(N_HEADS):
        qh = qf[:, h * QK_PAD:(h + 1) * QK_PAD]
        r = lax.rsqrt(jnp.sum(qh * qh, axis=-1, keepdims=True) * (1.0 / QK_DIM) + EPS)
        qn = qh * r * qg
        q_ref[h, :, 0:QK_NOPE] = (qn[:, 0:QK_NOPE] * scale).astype(_BF16)
        q_ref[h, :, QK_NOPE:QK_PAD] = (_rope(qn[:, QK_NOPE:QK_PAD], c, s1, s2) * scale).astype(_BF16)
        kn = kvf[:, h * 256:h * 256 + QK_NOPE]
        rk = lax.rsqrt((jnp.sum(kn * kn, axis=-1, keepdims=True) + kr_ss) * (1.0 / QK_DIM) + EPS)
        k_ref[h, :, 0:QK_NOPE] = (kn * rk * kgn_ref[...]).astype(_BF16)
        k_ref[h, :, QK_NOPE:QK_PAD] = (kr_roped * rk).astype(_BF16)
        v_ref[h, :, 0:V_DIM] = kvf[:, h * 256 + QK_NOPE:(h + 1) * 256].astype(_BF16)
        v_ref[h, :, V_DIM:V_PAD] = ones_col


def _mla_prep(pq, pkv, qag, kvag, wuq, wukv, qg, kgn, kgr, rope_c, rope_s1, rope_s2):
    row = lambda n: pl.BlockSpec((TM, n), lambda i: (i, 0))
    full = lambda a, b: pl.BlockSpec((a, b), lambda i: (0, 0))
    head = lambda n: pl.BlockSpec((N_HEADS, TM, n), lambda i: (0, i, 0))
    return pl.pallas_call(
        _mla_prep_kernel,
        grid=(NT_ALL,),
        in_specs=[row(Q_RANK), row(KVR_PAD), full(1, Q_RANK), full(1, KV_RANK),
                  full(Q_RANK, N_HEADS * QK_PAD), full(KV_RANK, N_HEADS * 256),
                  full(1, QK_PAD), full(1, QK_NOPE), full(1, 128),
                  row(128), row(128), row(128)],
        out_specs=[head(QK_PAD), head(QK_PAD), head(V_PAD)],
        out_shape=[jax.ShapeDtypeStruct((N_HEADS, T_ALL, QK_PAD), _BF16),
                   jax.ShapeDtypeStruct((N_HEADS, T_ALL, QK_PAD), _BF16),
                   jax.ShapeDtypeStruct((N_HEADS, T_ALL, V_PAD), _BF16)],
        compiler_params=_cparams(("parallel",)),
        name="mla_prep",
    )(pq, pkv, qag, kvag, wuq, wukv, qg, kgn, kgr, rope_c, rope_s1, rope_s2)


def _attn_kernel(q_ref, k_ref, v_ref, o_ref, sa_ref, sb_ref, acc_ref, *, n_keys, tk):
    q = q_ref[...]
    n_chunks = n_keys // tk
    assert n_chunks % 2 == 1

    def chunk(j):
        return pl.ds(j * tk if isinstance(j, int) else pl.multiple_of(j * tk, tk), tk)

    def scores(j, s_ref, m_cur):
        s = lax.dot_general(q, k_ref[chunk(j), :], (((1,), (1,)), ((), ())), preferred_element_type=_F32)
        s_ref[...] = s
        m = jnp.max(s, axis=-1, keepdims=True)
        return m if m_cur is None else jnp.maximum(m_cur, m)

    def accumulate(j, s_ref, m_prev, m_cur):
        p = jnp.exp2(s_ref[...] - m_cur)
        pv = jnp.dot(p.astype(_BF16), v_ref[chunk(j), :], preferred_element_type=_F32)
        acc_ref[...] = jnp.exp2(m_prev - m_cur) * acc_ref[...] + pv

    def step(i, carry):
        m_prev, m_cur = carry
        j = 2 * i
        m_1 = scores(j + 1, sb_ref, m_cur)
        accumulate(j, sa_ref, m_prev, m_cur)
        m_2 = scores(j + 2, sa_ref, m_1)
        accumulate(j + 1, sb_ref, m_cur, m_1)
        return m_1, m_2

    m0 = scores(0, sa_ref, None)
    acc_ref[...] = jnp.zeros_like(acc_ref)
    m_prev, m_cur = lax.fori_loop(0, (n_chunks - 1) // 2, step, (m0, m0), unroll=True)
    accumulate(n_chunks - 1, sa_ref, m_prev, m_cur)
    acc = acc_ref[...]
    o_ref[...] = acc[:, 0:V_DIM] / acc[:, V_DIM:V_DIM + 1]


def _attention(q, k, v, *, q_tile0, n_q, tq, key_tile0, n_keys, tk):
    return pl.pallas_call(
        functools.partial(_attn_kernel, n_keys=n_keys, tk=tk),
        grid=(N_HEADS, n_q // tq),
        in_specs=[
            pl.BlockSpec((None, tq, QK_PAD), lambda h, i: (h, q_tile0 + i, 0)),
            pl.BlockSpec((None, n_keys, QK_PAD), lambda h, i: (h, key_tile0, 0)),
            pl.BlockSpec((None, n_keys, V_PAD), lambda h, i: (h, key_tile0, 0)),
        ],
        out_specs=pl.BlockSpec((tq, V_DIM), lambda h, i: (i, h)),
        out_shape=jax.ShapeDtypeStruct((n_q, D_ATTN), _F32),
        scratch_shapes=[pltpu.VMEM((tq, tk), _F32), pltpu.VMEM((tq, tk), _F32), pltpu.VMEM((tq, V_PAD), _F32)],
        compiler_params=_cparams(("parallel", "parallel")),
        name="attention",
    )(q, k, v)


def _gelu_tanh(x):
    return 0.5 * x * (1.0 + jnp.tanh(math.sqrt(2.0 / math.pi) * (x + 0.044715 * (x * x * x))))


def _scan_tile(a, b, carry, reverse):
    order = list(range(N_GRP - 1, -1, -1) if reverse else range(N_GRP))
    hs, ps = [None] * N_GRP, [None] * N_GRP
    h = p = None
    for v in order:
        av, bv = a[v * 8:(v + 1) * 8, :], b[v * 8:(v + 1) * 8, :]
        h = bv if h is None else av * h + bv
        p = av if p is None else av * p
        hs[v], ps[v] = h, p
    cin = [None] * 8
    for s in (range(7, -1, -1) if reverse else range(8)):
        cin[s] = carry
        carry = p[s:s + 1, :] * carry + h[s:s + 1, :]
    cin = jnp.concatenate(cin, axis=0)
    return jnp.concatenate([hs[v] + ps[v] * cin for v in range(N_GRP)], axis=0), carry


def _cat_lanes(ref, rows=slice(None)):
    return jnp.concatenate([ref[c, rows, :] for c in range(RNN_BLOCKS)], axis=-1)


def _split_lanes(ref, val):
    for c in range(RNN_BLOCKS):
        ref[c] = val[:, c * RNN_BLOCK:(c + 1) * RNN_BLOCK]


def _time_perm(to_strided):
    r = lax.broadcasted_iota(jnp.int32, (TM, TM), 0)
    c = lax.broadcasted_iota(jnp.int32, (TM, TM), 1)
    if to_strided:
        src = (r % 8) * N_GRP + r // 8
    else:
        src = (r % N_GRP) * 8 + r // N_GRP
    return jnp.where(c == src, 1.0, 0.0).astype(_BF16)


def _rnn_kernel(rx_ref, rxp_ref, rxn_ref, cw_ref, cb_ref, wa_ref, ba_ref, wx_ref, bx_ref, lam_ref,
                *rest, reverse, final):
    if final:
        hf_ref, rg_ref, g_ref, o_ref, carry_ref = rest
    else:
        o_ref, carry_ref = rest
    j = pl.program_id(0)
    tile = jnp.where(j == 0, CTX_TILE, (NT_LAT - j) if reverse else (j - 1))

    @pl.when(j == 0)
    def _():
        carry_ref[...] = jnp.zeros_like(carry_ref)

    has_prev = jnp.logical_and(tile >= 1, tile < NT_LAT).astype(_F32)
    has_next = (tile < NT_LAT - 1).astype(_F32)
    p6 = _cat_lanes(rxp_ref, slice(7, 8)) * has_prev
    p7 = _cat_lanes(rxp_ref, slice(15, 16)) * has_prev
    n0 = _cat_lanes(rxn_ref, slice(0, 1)) * has_next
    cur = _cat_lanes(rx_ref)
    sub = lax.broadcasted_iota(jnp.int32, (8, 1), 0)
    e_m1 = jnp.where(sub == 0, p7, pltpu.roll(cur[TM - 8:TM, :], 1, axis=0))
    e_m2 = jnp.where(sub == 0, p6, pltpu.roll(cur[TM - 16:TM - 8, :], 1, axis=0))
    e_p1 = jnp.where(sub == 7, n0, pltpu.roll(cur[0:8, :], 7, axis=0))
    um1 = jnp.concatenate([e_m1, cur[0:TM - 8, :]], axis=0)
    um2 = jnp.concatenate([e_m2, e_m1, cur[0:TM - 16, :]], axis=0)
    up1 = jnp.concatenate([cur[8:TM, :], e_p1], axis=0)
    u = (cw_ref[0:1, :] * um2 + cw_ref[1:2, :] * um1 + cw_ref[2:3, :] * cur + cw_ref[3:4, :] * up1
         + cb_ref[...])

    ub = u.astype(_BF16)
    r_parts, i_parts = [], []
    for h in range(RNN_BLOCKS):
        uh = ub[:, h * RNN_BLOCK:(h + 1) * RNN_BLOCK]
        r_parts.append(jnp.dot(uh, wa_ref[h], preferred_element_type=_F32))
        i_parts.append(jnp.dot(uh, wx_ref[h], preferred_element_type=_F32))
    r = jax.nn.sigmoid(jnp.concatenate(r_parts, axis=-1) + ba_ref[...])
    i = jax.nn.sigmoid(jnp.concatenate(i_parts, axis=-1) + bx_ref[...])
    nl = -lam_ref[...]
    softplus = jnp.maximum(nl, 0.0) + jnp.log1p(jnp.exp(-jnp.abs(nl)))
    a = jnp.exp((-LRU_C) * r * softplus)
    b = jnp.sqrt(1.0 - a * a) * (i * u)

    h, carry = _scan_tile(a, b, carry_ref[0:1, :], reverse)
    carry_ref[0:1, :] = carry
    if final:
        y = _rms(_gelu_tanh(_cat_lanes(rg_ref)) * (_cat_lanes(hf_ref) + h), g_ref[...]).astype(_BF16)
        o_ref[...] = jnp.dot(_time_perm(to_strided=False), y, preferred_element_type=_F32).astype(_BF16)
    else:
        _split_lanes(o_ref, h)


def _rnn(l, d, rx, conv_w, conv_b, wa, ba, wx, bx, lam, extra=None):
    reverse = d == 1
    final = extra is not None

    def tile_of(j):
        return jnp.where(j == 0, CTX_TILE, (NT_LAT - j) if reverse else (j - 1))

    row = pl.BlockSpec((RNN_BLOCKS, TM, RNN_BLOCK), lambda j: (0, tile_of(j), 0))
    halo = lambda n, f: pl.BlockSpec((RNN_BLOCKS, n, RNN_BLOCK), lambda j: (0, f(tile_of(j)), 0))
    vec = lambda a: pl.BlockSpec((None, None, 1, D_RNN), lambda j: (l, d, 0, 0))
    mat = pl.BlockSpec((None, None, RNN_BLOCKS, RNN_BLOCK, RNN_BLOCK), lambda j: (l, d, 0, 0, 0))
    in_specs = [
        row,
        halo(2 * HALO, lambda t: jnp.maximum(t * (TM // (2 * HALO)) - 1, 0)),
        halo(HALO, lambda t: jnp.minimum((t + 1) * (TM // HALO), T_ALL // HALO - 1)),
        pl.BlockSpec((None, CONV_W, D_RNN), lambda j: (l, 0, 0)),
        pl.BlockSpec((None, 1, D_RNN), lambda j: (l, 0, 0)),
        mat, vec(ba), mat, vec(bx), vec(lam),
    ]
    args = [rx, rx, rx, conv_w, conv_b.reshape(DEPTH, 1, D_RNN), wa, ba.reshape(DEPTH, 2, 1, D_RNN),
            wx, bx.reshape(DEPTH, 2, 1, D_RNN), lam.reshape(DEPTH, 2, 1, D_RNN)]
    if final:
        hf, rg, g = extra
        in_specs += [row, row, pl.BlockSpec((1, D_RNN), lambda j: (0, 0))]
        args += [hf, rg, g]
    return pl.pallas_call(
        functools.partial(_rnn_kernel, reverse=reverse, final=final),
        grid=(NT_ALL,),
        in_specs=in_specs,
        out_specs=pl.BlockSpec((TM, D_RNN), lambda j: (tile_of(j), 0)) if final else row,
        out_shape=(jax.ShapeDtypeStruct((T_ALL, D_RNN), _BF16) if final
                   else jax.ShapeDtypeStruct((RNN_BLOCKS, T_ALL, RNN_BLOCK), _F32)),
        scratch_shapes=[pltpu.VMEM((8, D_RNN), _F32)],
        compiler_params=_cparams(("arbitrary",)),
        name="rnn_bwd" if reverse else "rnn_fwd",
    )(*args)


def _out_proj_kernel(alat_ref, actx_ref, yr_ref, ag_ref, w_ref, xl_ref, xc_ref, mod_ref, g2_ref, wr_ref, br_ref,
                     xo_ref, h2_ref, ri_ref, rw_ref, cnt_ref, carry_ref):
    i = pl.program_id(0)

    @pl.when(i == 0)
    def _():
        carry_ref[...] = jnp.zeros_like(carry_ref)

    a = jnp.where(i == CTX_TILE, actx_ref[...], alat_ref[...])
    ya = _rms(a, ag_ref[...]).astype(_BF16)
    yr = yr_ref[...]
    mix = (jnp.dot(ya, w_ref[0:D_ATTN, :], preferred_element_type=_F32)
           + jnp.dot(yr, w_ref[D_ATTN:D_MODEL, :], preferred_element_type=_F32))
    x = jnp.where(i == CTX_TILE, xc_ref[...], xl_ref[...]) + mod_ref[2:3, :] * mix
    xo_ref[...] = x
    h2 = _rms(x, g2_ref[...]) * (1.0 + mod_ref[4:5, :]) + mod_ref[3:4, :]
    h2_ref[...] = h2

    h2_hi = h2.astype(_BF16)
    h2_lo = (h2 - h2_hi.astype(_F32)).astype(_BF16)
    hh = jnp.dot(h2_hi, wr_ref[...], preferred_element_type=_F32)
    lh = jnp.dot(h2_lo, wr_ref[:, 0:ROUTE_LANES], preferred_element_type=_F32)
    logits = hh[:, 0:ROUTE_LANES] + hh[:, ROUTE_LANES:2 * ROUTE_LANES] + lh + br_ref[...]
    lane = lax.broadcasted_iota(jnp.int32, (TM, ROUTE_LANES), 1)
    lane_f = lane.astype(_F32)
    big = jnp.float32(1 << 20)
    neg = -jnp.inf
    gl = jnp.where(lane < N_GROUPS, logits, neg)
    gmax = jnp.max(gl, axis=-1, keepdims=True)
    gsum = jnp.sum(jnp.exp(gl - gmax), axis=-1, keepdims=True)
    p_g = 1.0 / gsum
    g_idx = jnp.min(jnp.where(gl == gmax, lane_f, big), axis=-1, keepdims=True).astype(jnp.int32)
    lo = E_OFF + EXPERTS_PER_GROUP * g_idx
    el = jnp.where(jnp.logical_and(lane >= lo, lane < lo + EXPERTS_PER_GROUP), logits, neg)
    m1 = jnp.max(el, axis=-1, keepdims=True)
    i1 = jnp.min(jnp.where(el == m1, lane_f, big), axis=-1, keepdims=True).astype(jnp.int32)
    el2 = jnp.where(lane == i1, neg, el)
    m2 = jnp.max(el2, axis=-1, keepdims=True)
    i2 = jnp.min(jnp.where(el2 == m2, lane_f, big), axis=-1, keepdims=True).astype(jnp.int32)
    e21 = jnp.exp(m2 - m1)
    w1 = p_g / (1.0 + e21)
    w2 = p_g * e21 / (1.0 + e21)

    hot1 = lane == i1
    hot2 = lane == i2
    hot = jnp.logical_or(hot1, hot2).astype(_F32)
    rr = lax.broadcasted_iota(jnp.int32, (TM, TM), 0)
    cc = lax.broadcasted_iota(jnp.int32, (TM, TM), 1)
    tri = (cc < rr).astype(_BF16)
    before = jnp.dot(tri, hot.astype(_BF16), preferred_element_type=_F32) + carry_ref[0:1, :]
    rank1 = jnp.sum(jnp.where(hot1, before, 0.0), axis=-1, keepdims=True)
    rank2 = jnp.sum(jnp.where(hot2, before, 0.0), axis=-1, keepdims=True)
    carry = carry_ref[0:1, :] + jnp.sum(hot, axis=0, keepdims=True)
    carry_ref[0:1, :] = carry
    cnt_ref[...] = jnp.broadcast_to(carry, cnt_ref.shape).astype(jnp.int32)

    ri = jnp.where(lane == 0, (i1 - E_OFF).astype(_F32), jnp.where(lane == 1, (i2 - E_OFF).astype(_F32),
                   jnp.where(lane == 2, rank1, jnp.where(lane == 3, rank2, 0.0))))
    ri_ref[...] = ri.T[0:4, :].astype(jnp.int32)
    rw_ref[...] = jnp.where(lane == 0, w1, jnp.where(lane == 1, w2, 0.0))


def _out_proj(n_tiles, alat, actx, yr, ag, w, x_lat, x_ctx, ctx_blk, mod_l, g2, wr, br):
    row = lambda n: pl.BlockSpec((TM, n), lambda i: (i, 0))
    full = lambda a, b: pl.BlockSpec((a, b), lambda i: (0, 0))
    n_rows = n_tiles * TM
    return pl.pallas_call(
        _out_proj_kernel,
        grid=(n_tiles,),
        in_specs=[
            pl.BlockSpec((TM, D_ATTN), lambda i: (jnp.minimum(i, NT_LAT - 1), 0)),
            full(TM, D_ATTN),
            row(D_RNN), full(1, D_ATTN), full(D_MODEL, D_MODEL), *_lat_ctx_specs(ctx_blk),
            pl.BlockSpec((None, 6, D_MODEL), lambda i: (i // NT_LAT, 0, 0)),
            full(1, D_MODEL), full(D_MODEL, 2 * ROUTE_LANES), full(1, ROUTE_LANES),
        ],
        out_specs=[row(D_MODEL), row(D_MODEL), pl.BlockSpec((4, TM), lambda i: (0, i)), row(ROUTE_LANES),
                   full(8, ROUTE_LANES)],
        out_shape=[
            jax.ShapeDtypeStruct((n_rows, D_MODEL), _F32),
            jax.ShapeDtypeStruct((n_rows, D_MODEL), _F32),
            jax.ShapeDtypeStruct((4, n_rows), jnp.int32),
            jax.ShapeDtypeStruct((n_rows, ROUTE_LANES), _F32),
            jax.ShapeDtypeStruct((8, ROUTE_LANES), jnp.int32),
        ],
        scratch_shapes=[pltpu.VMEM((8, ROUTE_LANES), _F32)],
        compiler_params=_cparams(("arbitrary",)),
        name="out_proj",
    )(alat, actx, yr, ag, w, x_lat, x_ctx, mod_l, g2, wr, br)


ROW_DMA_UNROLL = 8


def _row_dmas(n_tok, tile, start_ref, ri_ref, make_copy):
    def copy(t, k):
        tok = tile * TM + t
        slot = start_ref[ri_ref[k * n_tok + tok]] + ri_ref[(2 + k) * n_tok + tok]
        return make_copy(t, k, slot)

    def issue(t, c):
        copy(t, 0).start()
        copy(t, 1).start()
        return c

    def drain(t, c):
        copy(t, 0).wait()
        copy(t, 1).wait()
        return c

    lax.fori_loop(0, TM, issue, 0, unroll=ROW_DMA_UNROLL)
    lax.fori_loop(0, TM, drain, 0, unroll=ROW_DMA_UNROLL)


def _dispatch_kernel(start_ref, ri_ref, h2_ref, xs_in_hbm, xs_hbm, sem, *, n_tok):
    del xs_in_hbm
    _row_dmas(n_tok, pl.program_id(0), start_ref, ri_ref,
              lambda t, k, slot: pltpu.make_async_copy(h2_ref.at[pl.ds(t, 1), :],
                                                       xs_hbm.at[pl.ds(slot, 1), :], sem))


def _dispatch(n_tiles, n_slots, pad_start, ri_flat, h2, xs0):
    return pl.pallas_call(
        functools.partial(_dispatch_kernel, n_tok=n_tiles * TM),
        grid_spec=pltpu.PrefetchScalarGridSpec(
            num_scalar_prefetch=2,
            grid=(n_tiles,),
            in_specs=[pl.BlockSpec((TM, D_MODEL), lambda i, ps, ri: (i, 0)), pl.BlockSpec(memory_space=pl.ANY)],
            out_specs=pl.BlockSpec(memory_space=pl.ANY),
            scratch_shapes=[pltpu.SemaphoreType.DMA(())],
        ),
        out_shape=jax.ShapeDtypeStruct((n_slots, D_MODEL), _F32),
        input_output_aliases={3: 0},
        compiler_params=_cparams(("arbitrary",)),
        name="moe_dispatch",
    )(pad_start, ri_flat, h2, xs0)


def _expert_kernel(blk_e_ref, n_used_ref, x_ref, wgu_ref, wd_ref, y_ref):
    del blk_e_ref
    used = pl.program_id(0) < n_used_ref[0]

    @pl.when(used)
    def _():
        gu = jnp.dot(x_ref[...], wgu_ref[...], preferred_element_type=_F32)
        g, u = gu[:, 0:D_EXPERT], gu[:, D_EXPERT:2 * D_EXPERT]
        act = g * jax.nn.sigmoid(g) * u
        y_ref[...] = jnp.dot(act, wd_ref[...], preferred_element_type=_F32)

    @pl.when(jnp.logical_not(used))
    def _():
        y_ref[...] = jnp.zeros_like(y_ref)


def _experts(l, n_blocks, blk_e, n_used, xs, w_gate_up, w_down):
    blk = lambda b, be, nu: (jnp.minimum(b, nu[0] - 1), 0)
    return pl.pallas_call(
        _expert_kernel,
        grid_spec=pltpu.PrefetchScalarGridSpec(
            num_scalar_prefetch=2,
            grid=(n_blocks,),
            in_specs=[
                pl.BlockSpec((MOE_BLK, D_MODEL), blk),
                pl.BlockSpec((None, None, D_MODEL, 2 * D_EXPERT), lambda b, be, nu: (l, be[b], 0, 0)),
                pl.BlockSpec((None, None, D_EXPERT, D_MODEL), lambda b, be, nu: (l, be[b], 0, 0)),
            ],
            out_specs=pl.BlockSpec((MOE_BLK, D_MODEL), lambda b, be, nu: (b, 0)),
        ),
        out_shape=jax.ShapeDtypeStruct((n_blocks * MOE_BLK, D_MODEL), _F32),
        compiler_params=_cparams(("arbitrary",)),
        name="moe_experts",
    )(blk_e, n_used, xs, w_gate_up, w_down)


def _combine_kernel(start_ref, ri_ref, ys_hbm, x_ref, rw_ref, mod_ref, o_ref, buf, sem, *, n_tok):
    _row_dmas(n_tok, pl.program_id(0), start_ref, ri_ref,
              lambda t, k, slot: pltpu.make_async_copy(ys_hbm.at[pl.ds(slot, 1), :],
                                                       buf.at[k, pl.ds(t, 1), :], sem))
    rw = rw_ref[...]
    f = buf[0] * rw[:, 0:1] + buf[1] * rw[:, 1:2]
    o_ref[...] = x_ref[...] + mod_ref[5:6, :] * f


def _combine(n_tiles, pad_start, ri_flat, ys, x, rw, mod_l):
    row = lambda n: pl.BlockSpec((TM, n), lambda i, ps, ri: (i, 0))
    return pl.pallas_call(
        functools.partial(_combine_kernel, n_tok=n_tiles * TM),
        grid_spec=pltpu.PrefetchScalarGridSpec(
            num_scalar_prefetch=2,
            grid=(n_tiles,),
            in_specs=[
                pl.BlockSpec(memory_space=pl.ANY),
                row(D_MODEL), row(ROUTE_LANES),
                pl.BlockSpec((None, 6, D_MODEL), lambda i, ps, ri: (i // NT_LAT, 0, 0)),
            ],
            out_specs=row(D_MODEL),
            scratch_shapes=[pltpu.VMEM((2, TM, D_MODEL), _F32), pltpu.SemaphoreType.DMA(())],
        ),
        out_shape=jax.ShapeDtypeStruct((n_tiles * TM, D_MODEL), _F32),
        compiler_params=_cparams(("arbitrary",)),
        name="moe_combine",
    )(pad_start, ri_flat, ys, x, rw, mod_l)


MOE_BLOCKS = T_ALL * 2 // MOE_BLK + N_EXPERTS


def _moe(l, n_tiles, h2, ri, rw, counts, x, mod_l, w_gate_up, w_down, slots):
    n_blocks = MOE_BLOCKS
    cnt = counts[0, E_OFF:E_OFF + N_EXPERTS]
    padded = (cnt + MOE_BLK - 1) // MOE_BLK * MOE_BLK
    pad_end = jnp.cumsum(padded).astype(jnp.int32)
    pad_start = pad_end - padded
    blk_start = jnp.arange(n_blocks, dtype=jnp.int32) * MOE_BLK
    blk_e = jnp.minimum(jnp.sum((pad_end[None, :] <= blk_start[:, None]).astype(jnp.int32), axis=1),
                        N_EXPERTS - 1).astype(jnp.int32)
    n_used = pad_end[N_EXPERTS - 1:N_EXPERTS] // MOE_BLK
    ri_flat = ri.reshape(-1)
    xs = _dispatch(n_tiles, n_blocks * MOE_BLK, pad_start, ri_flat, h2, slots)
    ys = _experts(l, n_blocks, blk_e, n_used, xs, w_gate_up, w_down)
    return _combine(n_tiles, pad_start, ri_flat, ys, x, rw, mod_l), ys


def _rope_tables():
    pos = jnp.arange(SEQ, dtype=_F32)
    row = jnp.floor(pos / GRID_W)
    col = pos - row * GRID_W
    n_freq = QK_ROPE // 4
    inv = ROPE_THETA ** (-jnp.arange(n_freq, dtype=_F32) / n_freq)
    ang = jnp.concatenate([row[:, None] * inv, col[:, None] * inv], axis=-1)
    cos, sin = jnp.cos(ang), jnp.sin(ang)
    z32 = jnp.zeros((SEQ, 32), _F32)
    z64 = jnp.zeros((SEQ, 64), _F32)
    c = jnp.concatenate([cos, cos, jnp.ones((SEQ, 64), _F32)], axis=-1)
    s1 = jnp.concatenate([-sin, z32, z64], axis=-1)
    s2 = jnp.concatenate([z32, sin, z64], axis=-1)
    ctx_c = jnp.ones((CTX_LEN, 128), _F32)
    ctx_s = jnp.zeros((CTX_LEN, 128), _F32)
    return (jnp.concatenate([c, ctx_c]), jnp.concatenate([s1, ctx_s]), jnp.concatenate([s2, ctx_s]))


def _pad_head_gain(g):
    return jnp.pad(g, ((0, 0), (0, QK_PAD - QK_DIM)))[:, None, :]


def kernel(x, c, ctx, c_ctx, w_mod, b_mod, norm1_g, norm2_g, w_in, q_a_norm_g, kv_a_norm_g, w_uq, w_ukv, q_norm_g, k_norm_g, conv_w, conv_b, lru_wa, lru_ba, lru_wx, lru_bx, lru_lambda, attn_out_norm_g, rnn_out_norm_g, w_out, router_group_w, router_group_b, router_expert_w, router_expert_b, w_gate_up, w_down):
    L = DEPTH
    zc = jnp.zeros((L, D_MODEL, 64), _F32)
    w_in_p = jnp.concatenate([w_in[:, :, 0:832], zc, w_in[:, :, 832:2880]], axis=-1).astype(_BF16)
    w_uq_p = jnp.pad(w_uq.reshape(L, Q_RANK, N_HEADS, QK_DIM), ((0, 0), (0, 0), (0, 0), (0, QK_PAD - QK_DIM)))
    w_uq_p = w_uq_p.reshape(L, Q_RANK, N_HEADS * QK_PAD).astype(_BF16)
    w_ukv_b = w_ukv.astype(_BF16)
    w_out_b = w_out.astype(_BF16)
    qg_p = _pad_head_gain(q_norm_g)
    kgn = k_norm_g[:, None, 0:QK_NOPE]
    kgr = jnp.pad(k_norm_g[:, QK_NOPE:QK_DIM], ((0, 0), (0, 64)))[:, None, :]
    wr = jnp.pad(jnp.concatenate([router_group_w, router_expert_w], axis=-1),
                 ((0, 0), (0, 0), (0, ROUTE_LANES - N_GROUPS - N_EXPERTS)))
    wr_hi = wr.astype(_BF16)
    wr = jnp.concatenate([wr_hi, (wr - wr_hi.astype(_F32)).astype(_BF16)], axis=-1)
    br = jnp.pad(jnp.concatenate([router_group_b, router_expert_b], axis=-1),
                 ((0, 0), (0, ROUTE_LANES - N_GROUPS - N_EXPERTS)))[:, None, :]
    rope_c, rope_s1, rope_s2 = _rope_tables()

    cond_t = jnp.stack([c[0], c_ctx], axis=-1)
    mod = _adaln(cond_t, w_mod, b_mod).reshape(L, 2, 6, D_MODEL)

    x_lat, x_ctx, ctx_blk = x[0], ctx[0], 0
    slots = jnp.zeros((MOE_BLOCKS * MOE_BLK, D_MODEL), _F32)
    for l in range(L):
        last = l == L - 1
        n_tiles = NT_LAT if last else NT_ALL
        pq, pkv, rx, rg = _proj_in(x_lat, x_ctx, ctx_blk, mod[l], norm1_g[l][None, :], w_in_p[l])
        q, k, v = _mla_prep(pq, pkv, q_a_norm_g[l][None, :], kv_a_norm_g[l][None, :], w_uq_p[l], w_ukv_b[l],
                            qg_p[l], kgn[l], kgr[l], rope_c, rope_s1, rope_s2)
        attn_lat = _attention(q, k, v, q_tile0=0, n_q=SEQ, tq=TQ, key_tile0=0, n_keys=T_ALL, tk=TM * 11)
        if last:
            attn_ctx = attn_lat[0:TM]
        else:
            attn_ctx = _attention(q, k, v, q_tile0=CTX_TILE, n_q=CTX_LEN, tq=CTX_LEN, key_tile0=CTX_TILE,
                                  n_keys=CTX_LEN, tk=CTX_LEN)
        h_f = _rnn(l, 0, rx, conv_w, conv_b, lru_wa, lru_ba, lru_wx, lru_bx, lru_lambda)
        y_rnn = _rnn(l, 1, rx, conv_w, conv_b, lru_wa, lru_ba, lru_wx, lru_bx, lru_lambda,
                     extra=(h_f, rg, rnn_out_norm_g[l][None, :]))
        x_mid, h2, ri, rw, counts = _out_proj(n_tiles, attn_lat, attn_ctx, y_rnn, attn_out_norm_g[l][None, :],
                                              w_out_b[l], x_lat, x_ctx, ctx_blk, mod[l], norm2_g[l][None, :],
                                              wr[l], br[l])
        x_lat, slots = _moe(l, n_tiles, h2, ri, rw, counts, x_mid, mod[l], w_gate_up, w_down, slots)
        x_ctx, ctx_blk = x_lat, CTX_TILE
    return x_lat.reshape(1, SEQ, D_MODEL)
```

```python
import functools
import math

import jax
import jax.numpy as jnp
from jax import lax
from jax.experimental import pallas as pl
from jax.experimental.pallas import tpu as pltpu

D_MODEL = 2048
SEQ = 8192
CTX_LEN = 256
DEPTH = 4
GRID_W = 64
N_HEADS = 8
D_ATTN = 1024
V_DIM = 128
QK_NOPE = 128
QK_ROPE = 64
QK_DIM = 192
Q_RANK = 512
KV_RANK = 256
D_RNN = 1024
RNN_BLOCKS = 8
RNN_BLOCK = 128
CONV_W = 4
LRU_C = 8.0
N_GROUPS = 4
EXPERTS_PER_GROUP = 8
N_EXPERTS = 32
D_EXPERT = 512
ROPE_THETA = 10000.0
EPS = 1e-6

T_ALL = SEQ + CTX_LEN
TM = 256
NT_ALL = T_ALL // TM
NT_LAT = SEQ // TM
CTX_TILE = NT_LAT
HALO = 8
N_GRP = TM // 8
QK_PAD = 256
V_PAD = 256
KVR_PAD = 384
D_IN_PAD = Q_RANK + KVR_PAD + 2 * D_RNN
ROUTE_LANES = 128
E_OFF = N_GROUPS
MOE_BLK = 256
TQ = 1024
TK = 512
VMEM_LIMIT = 56 * 1024 * 1024

_F32 = jnp.float32
_BF16 = jnp.bfloat16


def _cparams(sem):
    return pltpu.CompilerParams(dimension_semantics=sem, vmem_limit_bytes=VMEM_LIMIT)


def _rms(x, g):
    return x * lax.rsqrt(jnp.mean(x * x, axis=-1, keepdims=True) + EPS) * g


def _adaln_kernel(cond_ref, w_ref, b_ref, o_ref):
    cond = cond_ref[...]
    s = cond * jax.nn.sigmoid(cond)
    w = w_ref[...]
    b = b_ref[...]
    o_ref[0:1, :] = jnp.sum(s[:, 0:1] * w, axis=0, keepdims=True) + b
    o_ref[1:2, :] = jnp.sum(s[:, 1:2] * w, axis=0, keepdims=True) + b


def _adaln(cond_t, w_mod, b_mod):
    tn = 1024
    n_col = 6 * D_MODEL // tn
    return pl.pallas_call(
        _adaln_kernel,
        grid=(DEPTH, n_col),
        in_specs=[
            pl.BlockSpec((D_MODEL, 2), lambda l, j: (0, 0)),
            pl.BlockSpec((None, D_MODEL, tn), lambda l, j: (l, 0, j)),
            pl.BlockSpec((None, 1, tn), lambda l, j: (l, 0, j)),
        ],
        out_specs=pl.BlockSpec((None, 2, tn), lambda l, j: (l, 0, j)),
        out_shape=jax.ShapeDtypeStruct((DEPTH, 2, 6 * D_MODEL), _F32),
        compiler_params=_cparams(("parallel", "parallel")),
        name="adaln",
    )(cond_t, w_mod, b_mod.reshape(DEPTH, 1, 6 * D_MODEL))


def _proj_in_kernel(xl_ref, xc_ref, mod_ref, g_ref, w_ref, pq_ref, pkv_ref, rx_ref, rg_ref):
    x = jnp.where(pl.program_id(0) == CTX_TILE, xc_ref[...], xl_ref[...])
    h = _rms(x, g_ref[...]) * (1.0 + mod_ref[1:2, :]) + mod_ref[0:1, :]
    hb = h.astype(_BF16)
    o0, o1, o2 = Q_RANK, Q_RANK + KVR_PAD, Q_RANK + KVR_PAD + D_RNN
    pq_ref[...] = jnp.dot(hb, w_ref[:, 0:o0], preferred_element_type=_F32)
    pkv_ref[...] = jnp.dot(hb, w_ref[:, o0:o1], preferred_element_type=_F32)
    hs = jnp.dot(_time_perm(to_strided=True), hb, preferred_element_type=_F32).astype(_BF16)
    _split_lanes(rx_ref, jnp.dot(hs, w_ref[:, o1:o2], preferred_element_type=_F32))
    _split_lanes(rg_ref, jnp.dot(hs, w_ref[:, o2:D_IN_PAD], preferred_element_type=_F32))


def _lat_ctx_specs(ctx_blk):
    return [pl.BlockSpec((TM, D_MODEL), lambda i: (jnp.minimum(i, NT_LAT - 1), 0)),
            pl.BlockSpec((TM, D_MODEL), lambda i: (ctx_blk, 0))]


def _proj_in(x_lat, x_ctx, ctx_blk, mod_l, g, w):
    row = lambda n: pl.BlockSpec((TM, n), lambda i: (i, 0))
    return pl.pallas_call(
        _proj_in_kernel,
        grid=(NT_ALL,),
        in_specs=_lat_ctx_specs(ctx_blk) + [
            pl.BlockSpec((None, 6, D_MODEL), lambda i: (i // NT_LAT, 0, 0)),
            pl.BlockSpec((1, D_MODEL), lambda i: (0, 0)),
            pl.BlockSpec((D_MODEL, D_IN_PAD), lambda i: (0, 0)),
        ],
        out_specs=[row(Q_RANK), row(KVR_PAD)] + [pl.BlockSpec((RNN_BLOCKS, TM, RNN_BLOCK), lambda i: (0, i, 0))] * 2,
        out_shape=[
            jax.ShapeDtypeStruct((T_ALL, Q_RANK), _F32),
            jax.ShapeDtypeStruct((T_ALL, KVR_PAD), _F32),
            jax.ShapeDtypeStruct((RNN_BLOCKS, T_ALL, RNN_BLOCK), _F32),
            jax.ShapeDtypeStruct((RNN_BLOCKS, T_ALL, RNN_BLOCK), _F32),
        ],
        compiler_params=_cparams(("parallel",)),
        name="proj_in",
    )(x_lat, x_ctx, mod_l, g, w)


def _rope(y, c, s1, s2):
    return y * c + pltpu.roll(y, 96, axis=1) * s1 + pltpu.roll(y, 32, axis=1) * s2


def _mla_prep_kernel(pq_ref, pkv_ref, qag_ref, kvag_ref, wuq_ref, wukv_ref, qg_ref, kgn_ref, kgr_ref,
                     c_ref, s1_ref, s2_ref, q_ref, k_ref, v_ref):
    c, s1, s2 = c_ref[...], s1_ref[...], s2_ref[...]
    scale = QK_DIM ** -0.5 * math.log2(math.e)
    qa = _rms(pq_ref[...], qag_ref[...]).astype(_BF16)
    qf = jnp.dot(qa, wuq_ref[...], preferred_element_type=_F32)
    pkv = pkv_ref[...]
    kva = _rms(pkv[:, 0:KV_RANK], kvag_ref[...]).astype(_BF16)
    kvf = jnp.dot(kva, wukv_ref[...], preferred_element_type=_F32)
    kr = pkv[:, KV_RANK:KVR_PAD]
    kr_ss = jnp.sum(kr * kr, axis=-1, keepdims=True)
    kr_roped = _rope(kr * kgr_ref[...], c, s1, s2)
    qg = qg_ref[...]
    lane = lax.broadcasted_iota(jnp.int32, (TM, V_PAD - V_DIM), 1)
    ones_col = jnp.where(lane == 0, 1.0, 0.0).astype(_BF16)
    for h in range(N_HEADS):
        qh = qf[:, h * QK_PAD:(h + 1) * QK_PAD]
        r = lax.rsqrt(jnp.sum(qh * qh, axis=-1, keepdims=True) * (1.0 / QK_DIM) + EPS)
        qn = qh * r * qg
        q_ref[h, :, 0:QK_NOPE] = (qn[:, 0:QK_NOPE] * scale).astype(_BF16)
        q_ref[h, :, QK_NOPE:QK_PAD] = (_rope(qn[:, QK_NOPE:QK_PAD], c, s1, s2) * scale).astype(_BF16)
        kn = kvf[:, h * 256:h * 256 + QK_NOPE]
        rk = lax.rsqrt((jnp.sum(kn * kn, axis=-1, keepdims=True) + kr_ss) * (1.0 / QK_DIM) + EPS)
        k_ref[h, :, 0:QK_NOPE] = (kn * rk * kgn_ref[...]).astype(_BF16)
        k_ref[h, :, QK_NOPE:QK_PAD] = (kr_roped * rk).astype(_BF16)
        v_ref[h, :, 0:V_DIM] = kvf[:, h * 256 + QK_NOPE:(h + 1) * 256].astype(_BF16)
        v_ref[h, :, V_DIM:V_PAD] = ones_col


def _mla_prep(pq, pkv, qag, kvag, wuq, wukv, qg, kgn, kgr, rope_c, rope_s1, rope_s2):
    row = lambda n: pl.BlockSpec((TM, n), lambda i: (i, 0))
    full = lambda a, b: pl.BlockSpec((a, b), lambda i: (0, 0))
    head = lambda n: pl.BlockSpec((N_HEADS, TM, n), lambda i: (0, i, 0))
    return pl.pallas_call(
        _mla_prep_kernel,
        grid=(NT_ALL,),
        in_specs=[row(Q_RANK), row(KVR_PAD), full(1, Q_RANK), full(1, KV_RANK),
                  full(Q_RANK, N_HEADS * QK_PAD), full(KV_RANK, N_HEADS * 256),
                  full(1, QK_PAD), full(1, QK_NOPE), full(1, 128),
                  row(128), row(128), row(128)],
        out_specs=[head(QK_PAD), head(QK_PAD), head(V_PAD)],
        out_shape=[jax.ShapeDtypeStruct((N_HEADS, T_ALL, QK_PAD), _BF16),
                   jax.ShapeDtypeStruct((N_HEADS, T_ALL, QK_PAD), _BF16),
                   jax.ShapeDtypeStruct((N_HEADS, T_ALL, V_PAD), _BF16)],
        compiler_params=_cparams(("parallel",)),
        name="mla_prep",
    )(pq, pkv, qag, kvag, wuq, wukv, qg, kgn, kgr, rope_c, rope_s1, rope_s2)


def _attn_kernel(q_ref, k_ref, v_ref, o_ref, sa_ref, sb_ref, acc_ref, *, n_keys, tk):
    q = q_ref[...]
    n_chunks = n_keys // tk
    assert n_chunks % 2 == 1

    def chunk(j):
        return pl.ds(j * tk if isinstance(j, int) else pl.multiple_of(j * tk, tk), tk)

    def scores(j, s_ref, m_cur):
        s = lax.dot_general(q, k_ref[chunk(j), :], (((1,), (1,)), ((), ())), preferred_element_type=_F32)
        s_ref[...] = s
        m = jnp.max(s, axis=-1, keepdims=True)
        return m if m_cur is None else jnp.maximum(m_cur, m)

    def accumulate(j, s_ref, m_prev, m_cur):
        p = jnp.exp2(s_ref[...] - m_cur)
        pv = jnp.dot(p.astype(_BF16), v_ref[chunk(j), :], preferred_element_type=_F32)
        acc_ref[...] = jnp.exp2(m_prev - m_cur) * acc_ref[...] + pv

    def step(i, carry):
        m_prev, m_cur = carry
        j = 2 * i
        m_1 = scores(j + 1, sb_ref, m_cur)
        accumulate(j, sa_ref, m_prev, m_cur)
        m_2 = scores(j + 2, sa_ref, m_1)
        accumulate(j + 1, sb_ref, m_cur, m_1)
        return m_1, m_2

    m0 = scores(0, sa_ref, None)
    acc_ref[...] = jnp.zeros_like(acc_ref)
    m_prev, m_cur = lax.fori_loop(0, (n_chunks - 1) // 2, step, (m0, m0), unroll=True)
    accumulate(n_chunks - 1, sa_ref, m_prev, m_cur)
    acc = acc_ref[...]
    o_ref[...] = acc[:, 0:V_DIM] / acc[:, V_DIM:V_DIM + 1]


def _attention(q, k, v, *, q_tile0, n_q, tq, key_tile0, n_keys, tk):
    return pl.pallas_call(
        functools.partial(_attn_kernel, n_keys=n_keys, tk=tk),
        grid=(N_HEADS, n_q // tq),
        in_specs=[
            pl.BlockSpec((None, tq, QK_PAD), lambda h, i: (h, q_tile0 + i, 0)),
            pl.BlockSpec((None, n_keys, QK_PAD), lambda h, i: (h, key_tile0, 0)),
            pl.BlockSpec((None, n_keys, V_PAD), lambda h, i: (h, key_tile0, 0)),
        ],
        out_specs=pl.BlockSpec((tq, V_DIM), lambda h, i: (i, h)),
        out_shape=jax.ShapeDtypeStruct((n_q, D_ATTN), _F32),
        scratch_shapes=[pltpu.VMEM((tq, tk), _F32), pltpu.VMEM((tq, tk), _F32), pltpu.VMEM((tq, V_PAD), _F32)],
        compiler_params=_cparams(("parallel", "parallel")),
        name="attention",
    )(q, k, v)


def _gelu_tanh(x):
    return 0.5 * x * (1.0 + jnp.tanh(math.sqrt(2.0 / math.pi) * (x + 0.044715 * (x * x * x))))


def _scan_tile(a, b, carry, reverse):
    order = list(range(N_GRP - 1, -1, -1) if reverse else range(N_GRP))
    hs, ps = [None] * N_GRP, [None] * N_GRP
    h = p = None
    for v in order:
        av, bv = a[v * 8:(v + 1) * 8, :], b[v * 8:(v + 1) * 8, :]
        h = bv if h is None else av * h + bv
        p = av if p is None else av * p
        hs[v], ps[v] = h, p
    cin = [None] * 8
    for s in (range(7, -1, -1) if reverse else range(8)):
        cin[s] = carry
        carry = p[s:s + 1, :] * carry + h[s:s + 1, :]
    cin = jnp.concatenate(cin, axis=0)
    return jnp.concatenate([hs[v] + ps[v] * cin for v in range(N_GRP)], axis=0), carry


def _cat_lanes(ref, rows=slice(None)):
    return jnp.concatenate([ref[c, rows, :] for c in range(RNN_BLOCKS)], axis=-1)


def _split_lanes(ref, val):
    for c in range(RNN_BLOCKS):
        ref[c] = val[:, c * RNN_BLOCK:(c + 1) * RNN_BLOCK]


def _time_perm(to_strided):
    r = lax.broadcasted_iota(jnp.int32, (TM, TM), 0)
    c = lax.broadcasted_iota(jnp.int32, (TM, TM), 1)
    if to_strided:
        src = (r % 8) * N_GRP + r // 8
    else:
        src = (r % N_GRP) * 8 + r // N_GRP
    return jnp.where(c == src, 1.0, 0.0).astype(_BF16)


def _rnn_kernel(rx_ref, rxp_ref, rxn_ref, cw_ref, cb_ref, wa_ref, ba_ref, wx_ref, bx_ref, lam_ref,
                *rest, reverse, final):
    if final:
        hf_ref, rg_ref, g_ref, o_ref, carry_ref = rest
    else:
        o_ref, carry_ref = rest
    j = pl.program_id(0)
    tile = jnp.where(j == 0, CTX_TILE, (NT_LAT - j) if reverse else (j - 1))

    @pl.when(j == 0)
    def _():
        carry_ref[...] = jnp.zeros_like(carry_ref)

    has_prev = jnp.logical_and(tile >= 1, tile < NT_LAT).astype(_F32)
    has_next = (tile < NT_LAT - 1).astype(_F32)
    p6 = _cat_lanes(rxp_ref, slice(7, 8)) * has_prev
    p7 = _cat_lanes(rxp_ref, slice(15, 16)) * has_prev
    n0 = _cat_lanes(rxn_ref, slice(0, 1)) * has_next
    cur = _cat_lanes(rx_ref)
    sub = lax.broadcasted_iota(jnp.int32, (8, 1), 0)
    e_m1 = jnp.where(sub == 0, p7, pltpu.roll(cur[TM - 8:TM, :], 1, axis=0))
    e_m2 = jnp.where(sub == 0, p6, pltpu.roll(cur[TM - 16:TM - 8, :], 1, axis=0))
    e_p1 = jnp.where(sub == 7, n0, pltpu.roll(cur[0:8, :], 7, axis=0))
    um1 = jnp.concatenate([e_m1, cur[0:TM - 8, :]], axis=0)
    um2 = jnp.concatenate([e_m2, e_m1, cur[0:TM - 16, :]], axis=0)
    up1 = jnp.concatenate([cur[8:TM, :], e_p1], axis=0)
    u = (cw_ref[0:1, :] * um2 + cw_ref[1:2, :] * um1 + cw_ref[2:3, :] * cur + cw_ref[3:4, :] * up1
         + cb_ref[...])

    ub = u.astype(_BF16)
    r_parts, i_parts = [], []
    for h in range(RNN_BLOCKS):
        uh = ub[:, h * RNN_BLOCK:(h + 1) * RNN_BLOCK]
        r_parts.append(jnp.dot(uh, wa_ref[h], preferred_element_type=_F32))
        i_parts.append(jnp.dot(uh, wx_ref[h], preferred_element_type=_F32))
    r = jax.nn.sigmoid(jnp.concatenate(r_parts, axis=-1) + ba_ref[...])
    i = jax.nn.sigmoid(jnp.concatenate(i_parts, axis=-1) + bx_ref[...])
    nl = -lam_ref[...]
    softplus = jnp.maximum(nl, 0.0) + jnp.log1p(jnp.exp(-jnp.abs(nl)))
    a = jnp.exp((-LRU_C) * r * softplus)
    b = jnp.sqrt(1.0 - a * a) * (i * u)

    h, carry = _scan_tile(a, b, carry_ref[0:1, :], reverse)
    carry_ref[0:1, :] = carry
    if final:
        y = _rms(_gelu_tanh(_cat_lanes(rg_ref)) * (_cat_lanes(hf_ref) + h), g_ref[...]).astype(_BF16)
        o_ref[...] = jnp.dot(_time_perm(to_strided=False), y, preferred_element_type=_F32).astype(_BF16)
    else:
        _split_lanes(o_ref, h)


def _rnn(l, d, rx, conv_w, conv_b, wa, ba, wx, bx, lam, extra=None):
    reverse = d == 1
    final = extra is not None

    def tile_of(j):
        return jnp.where(j == 0, CTX_TILE, (NT_LAT - j) if reverse else (j - 1))

    row = pl.BlockSpec((RNN_BLOCKS, TM, RNN_BLOCK), lambda j: (0, tile_of(j), 0))
    halo = lambda n, f: pl.BlockSpec((RNN_BLOCKS, n, RNN_BLOCK), lambda j: (0, f(tile_of(j)), 0))
    vec = lambda a: pl.BlockSpec((None, None, 1, D_RNN), lambda j: (l, d, 0, 0))
    mat = pl.BlockSpec((None, None, RNN_BLOCKS, RNN_BLOCK, RNN_BLOCK), lambda j: (l, d, 0, 0, 0))
    in_specs = [
        row,
        halo(2 * HALO, lambda t: jnp.maximum(t * (TM // (2 * HALO)) - 1, 0)),
        halo(HALO, lambda t: jnp.minimum((t + 1) * (TM // HALO), T_ALL // HALO - 1)),
        pl.BlockSpec((None, CONV_W, D_RNN), lambda j: (l, 0, 0)),
        pl.BlockSpec((None, 1, D_RNN), lambda j: (l, 0, 0)),
        mat, vec(ba), mat, vec(bx), vec(lam),
    ]
    args = [rx, rx, rx, conv_w, conv_b.reshape(DEPTH, 1, D_RNN), wa, ba.reshape(DEPTH, 2, 1, D_RNN),
            wx, bx.reshape(DEPTH, 2, 1, D_RNN), lam.reshape(DEPTH, 2, 1, D_RNN)]
    if final:
        hf, rg, g = extra
        in_specs += [row, row, pl.BlockSpec((1, D_RNN), lambda j: (0, 0))]
        args += [hf, rg, g]
    return pl.pallas_call(
        functools.partial(_rnn_kernel, reverse=reverse, final=final),
        grid=(NT_ALL,),
        in_specs=in_specs,
        out_specs=pl.BlockSpec((TM, D_RNN), lambda j: (tile_of(j), 0)) if final else row,
        out_shape=(jax.ShapeDtypeStruct((T_ALL, D_RNN), _BF16) if final
                   else jax.ShapeDtypeStruct((RNN_BLOCKS, T_ALL, RNN_BLOCK), _F32)),
        scratch_shapes=[pltpu.VMEM((8, D_RNN), _F32)],
        compiler_params=_cparams(("arbitrary",)),
        name="rnn_bwd" if reverse else "rnn_fwd",
    )(*args)


def _out_proj_kernel(alat_ref, actx_ref, yr_ref, ag_ref, w_ref, xl_ref, xc_ref, mod_ref, g2_ref, wr_ref, br_ref,
                     xo_ref, h2_ref, ri_ref, rw_ref, cnt_ref, carry_ref):
    i = pl.program_id(0)

    @pl.when(i == 0)
    def _():
        carry_ref[...] = jnp.zeros_like(carry_ref)

    a = jnp.where(i == CTX_TILE, actx_ref[...], alat_ref[...])
    ya = _rms(a, ag_ref[...]).astype(_BF16)
    yr = yr_ref[...]
    mix = (jnp.dot(ya, w_ref[0:D_ATTN, :], preferred_element_type=_F32)
           + jnp.dot(yr, w_ref[D_ATTN:D_MODEL, :], preferred_element_type=_F32))
    x = jnp.where(i == CTX_TILE, xc_ref[...], xl_ref[...]) + mod_ref[2:3, :] * mix
    xo_ref[...] = x
    h2 = _rms(x, g2_ref[...]) * (1.0 + mod_ref[4:5, :]) + mod_ref[3:4, :]
    h2_ref[...] = h2

    h2_hi = h2.astype(_BF16)
    h2_lo = (h2 - h2_hi.astype(_F32)).astype(_BF16)
    hh = jnp.dot(h2_hi, wr_ref[...], preferred_element_type=_F32)
    lh = jnp.dot(h2_lo, wr_ref[:, 0:ROUTE_LANES], preferred_element_type=_F32)
    logits = hh[:, 0:ROUTE_LANES] + hh[:, ROUTE_LANES:2 * ROUTE_LANES] + lh + br_ref[...]
    lane = lax.broadcasted_iota(jnp.int32, (TM, ROUTE_LANES), 1)
    lane_f = lane.astype(_F32)
    big = jnp.float32(1 << 20)
    neg = -jnp.inf
    gl = jnp.where(lane < N_GROUPS, logits, neg)
    gmax = jnp.max(gl, axis=-1, keepdims=True)
    gsum = jnp.sum(jnp.exp(gl - gmax), axis=-1, keepdims=True)
    p_g = 1.0 / gsum
    g_idx = jnp.min(jnp.where(gl == gmax, lane_f, big), axis=-1, keepdims=True).astype(jnp.int32)
    lo = E_OFF + EXPERTS_PER_GROUP * g_idx
    el = jnp.where(jnp.logical_and(lane >= lo, lane < lo + EXPERTS_PER_GROUP), logits, neg)
    m1 = jnp.max(el, axis=-1, keepdims=True)
    i1 = jnp.min(jnp.where(el == m1, lane_f, big), axis=-1, keepdims=True).astype(jnp.int32)
    el2 = jnp.where(lane == i1, neg, el)
    m2 = jnp.max(el2, axis=-1, keepdims=True)
    i2 = jnp.min(jnp.where(el2 == m2, lane_f, big), axis=-1, keepdims=True).astype(jnp.int32)
    e21 = jnp.exp(m2 - m1)
    w1 = p_g / (1.0 + e21)
    w2 = p_g * e21 / (1.0 + e21)

    hot1 = lane == i1
    hot2 = lane == i2
    hot = jnp.logical_or(hot1, hot2).astype(_F32)
    rr = lax.broadcasted_iota(jnp.int32, (TM, TM), 0)
    cc = lax.broadcasted_iota(jnp.int32, (TM, TM), 1)
    tri = (cc < rr).astype(_BF16)
    before = jnp.dot(tri, hot.astype(_BF16), preferred_element_type=_F32) + carry_ref[0:1, :]
    rank1 = jnp.sum(jnp.where(hot1, before, 0.0), axis=-1, keepdims=True)
    rank2 = jnp.sum(jnp.where(hot2, before, 0.0), axis=-1, keepdims=True)
    carry = carry_ref[0:1, :] + jnp.sum(hot, axis=0, keepdims=True)
    carry_ref[0:1, :] = carry
    cnt_ref[...] = jnp.broadcast_to(carry, cnt_ref.shape).astype(jnp.int32)

    ri = jnp.where(lane == 0, (i1 - E_OFF).astype(_F32), jnp.where(lane == 1, (i2 - E_OFF).astype(_F32),
                   jnp.where(lane == 2, rank1, jnp.where(lane == 3, rank2, 0.0))))
    ri_ref[...] = ri.T[0:4, :].astype(jnp.int32)
    rw_ref[...] = jnp.where(lane == 0, w1, jnp.where(lane == 1, w2, 0.0))


def _out_proj(n_tiles, alat, actx, yr, ag, w, x_lat, x_ctx, ctx_blk, mod_l, g2, wr, br):
    row = lambda n: pl.BlockSpec((TM, n), lambda i: (i, 0))
    full = lambda a, b: pl.BlockSpec((a, b), lambda i: (0, 0))
    n_rows = n_tiles * TM
    return pl.pallas_call(
        _out_proj_kernel,
        grid=(n_tiles,),
        in_specs=[
            pl.BlockSpec((TM, D_ATTN), lambda i: (jnp.minimum(i, NT_LAT - 1), 0)),
            full(TM, D_ATTN),
            row(D_RNN), full(1, D_ATTN), full(D_MODEL, D_MODEL), *_lat_ctx_specs(ctx_blk),
            pl.BlockSpec((None, 6, D_MODEL), lambda i: (i // NT_LAT, 0, 0)),
            full(1, D_MODEL), full(D_MODEL, 2 * ROUTE_LANES), full(1, ROUTE_LANES),
        ],
        out_specs=[row(D_MODEL), row(D_MODEL), pl.BlockSpec((4, TM), lambda i: (0, i)), row(ROUTE_LANES),
                   full(8, ROUTE_LANES)],
        out_shape=[
            jax.ShapeDtypeStruct((n_rows, D_MODEL), _F32),
            jax.ShapeDtypeStruct((n_rows, D_MODEL), _F32),
            jax.ShapeDtypeStruct((4, n_rows), jnp.int32),
            jax.ShapeDtypeStruct((n_rows, ROUTE_LANES), _F32),
            jax.ShapeDtypeStruct((8, ROUTE_LANES), jnp.int32),
        ],
        scratch_shapes=[pltpu.VMEM((8, ROUTE_LANES), _F32)],
        compiler_params=_cparams(("arbitrary",)),
        name="out_proj",
    )(alat, actx, yr, ag, w, x_lat, x_ctx, mod_l, g2, wr, br)


ROW_DMA_UNROLL = 8


def _row_dmas(n_tok, tile, start_ref, ri_ref, make_copy):
    def copy(t, k):
        tok = tile * TM + t
        slot = start_ref[ri_ref[k * n_tok + tok]] + ri_ref[(2 + k) * n_tok + tok]
        return make_copy(t, k, slot)

    def issue(t, c):
        copy(t, 0).start(priority=0)
        copy(t, 1).start(priority=1)
        return c

    def drain(t, c):
        copy(t, 0).wait()
        copy(t, 1).wait()
        return c

    lax.fori_loop(0, TM, issue, 0, unroll=ROW_DMA_UNROLL)
    lax.fori_loop(0, TM, drain, 0, unroll=ROW_DMA_UNROLL)


def _dispatch_kernel(start_ref, ri_ref, h2_ref, xs_in_hbm, xs_hbm, sem, *, n_tok):
    del xs_in_hbm
    _row_dmas(n_tok, pl.program_id(0), start_ref, ri_ref,
              lambda t, k, slot: pltpu.make_async_copy(h2_ref.at[pl.ds(t, 1), :],
                                                       xs_hbm.at[pl.ds(slot, 1), :], sem))


def _dispatch(n_tiles, n_slots, pad_start, ri_flat, h2, xs0):
    return pl.pallas_call(
        functools.partial(_dispatch_kernel, n_tok=n_tiles * TM),
        grid_spec=pltpu.PrefetchScalarGridSpec(
            num_scalar_prefetch=2,
            grid=(n_tiles,),
            in_specs=[pl.BlockSpec((TM, D_MODEL), lambda i, ps, ri: (i, 0)), pl.BlockSpec(memory_space=pl.ANY)],
            out_specs=pl.BlockSpec(memory_space=pl.ANY),
            scratch_shapes=[pltpu.SemaphoreType.DMA(())],
        ),
        out_shape=jax.ShapeDtypeStruct((n_slots, D_MODEL), _F32),
        input_output_aliases={3: 0},
        compiler_params=_cparams(("arbitrary",)),
        name="moe_dispatch",
    )(pad_start, ri_flat, h2, xs0)


def _expert_kernel(blk_e_ref, n_used_ref, x_ref, wgu_ref, wd_ref, y_ref):
    del blk_e_ref
    used = pl.program_id(0) < n_used_ref[0]

    @pl.when(used)
    def _():
        gu = jnp.dot(x_ref[...], wgu_ref[...], preferred_element_type=_F32)
        g, u = gu[:, 0:D_EXPERT], gu[:, D_EXPERT:2 * D_EXPERT]
        act = g * jax.nn.sigmoid(g) * u
        y_ref[...] = jnp.dot(act, wd_ref[...], preferred_element_type=_F32)

    @pl.when(jnp.logical_not(used))
    def _():
        y_ref[...] = jnp.zeros_like(y_ref)


def _experts(l, n_blocks, blk_e, n_used, xs, w_gate_up, w_down):
    blk = lambda b, be, nu: (jnp.minimum(b, nu[0] - 1), 0)
    return pl.pallas_call(
        _expert_kernel,
        grid_spec=pltpu.PrefetchScalarGridSpec(
            num_scalar_prefetch=2,
            grid=(n_blocks,),
            in_specs=[
                pl.BlockSpec((MOE_BLK, D_MODEL), blk),
                pl.BlockSpec((None, None, D_MODEL, 2 * D_EXPERT), lambda b, be, nu: (l, be[b], 0, 0)),
                pl.BlockSpec((None, None, D_EXPERT, D_MODEL), lambda b, be, nu: (l, be[b], 0, 0)),
            ],
            out_specs=pl.BlockSpec((MOE_BLK, D_MODEL), lambda b, be, nu: (b, 0)),
        ),
        out_shape=jax.ShapeDtypeStruct((n_blocks * MOE_BLK, D_MODEL), _F32),
        compiler_params=_cparams(("arbitrary",)),
        name="moe_experts",
    )(blk_e, n_used, xs, w_gate_up, w_down)


def _combine_kernel(start_ref, ri_ref, ys_hbm, x_ref, rw_ref, mod_ref, o_ref, buf, sem, *, n_tok):
    _row_dmas(n_tok, pl.program_id(0), start_ref, ri_ref,
              lambda t, k, slot: pltpu.make_async_copy(ys_hbm.at[pl.ds(slot, 1), :],
                                                       buf.at[k, pl.ds(t, 1), :], sem))
    rw = rw_ref[...]
    f = buf[0] * rw[:, 0:1] + buf[1] * rw[:, 1:2]
    o_ref[...] = x_ref[...] + mod_ref[5:6, :] * f


def _combine(n_tiles, pad_start, ri_flat, ys, x, rw, mod_l):
    row = lambda n: pl.BlockSpec((TM, n), lambda i, ps, ri: (i, 0))
    return pl.pallas_call(
        functools.partial(_combine_kernel, n_tok=n_tiles * TM),
        grid_spec=pltpu.PrefetchScalarGridSpec(
            num_scalar_prefetch=2,
            grid=(n_tiles,),
            in_specs=[
                pl.BlockSpec(memory_space=pl.ANY),
                row(D_MODEL), row(ROUTE_LANES),
                pl.BlockSpec((None, 6, D_MODEL), lambda i, ps, ri: (i // NT_LAT, 0, 0)),
            ],
            out_specs=row(D_MODEL),
            scratch_shapes=[pltpu.VMEM((2, TM, D_MODEL), _F32), pltpu.SemaphoreType.DMA(())],
        ),
        out_shape=jax.ShapeDtypeStruct((n_tiles * TM, D_MODEL), _F32),
        compiler_params=_cparams(("arbitrary",)),
        name="moe_combine",
    )(pad_start, ri_flat, ys, x, rw, mod_l)


MOE_BLOCKS = T_ALL * 2 // MOE_BLK + N_EXPERTS


def _moe(l, n_tiles, h2, ri, rw, counts, x, mod_l, w_gate_up, w_down, slots):
    n_blocks = MOE_BLOCKS
    cnt = counts[0, E_OFF:E_OFF + N_EXPERTS]
    padded = (cnt + MOE_BLK - 1) // MOE_BLK * MOE_BLK
    pad_end = jnp.cumsum(padded).astype(jnp.int32)
    pad_start = pad_end - padded
    blk_start = jnp.arange(n_blocks, dtype=jnp.int32) * MOE_BLK
    blk_e = jnp.minimum(jnp.sum((pad_end[None, :] <= blk_start[:, None]).astype(jnp.int32), axis=1),
                        N_EXPERTS - 1).astype(jnp.int32)
    n_used = pad_end[N_EXPERTS - 1:N_EXPERTS] // MOE_BLK
    ri_flat = ri.reshape(-1)
    xs = _dispatch(n_tiles, n_blocks * MOE_BLK, pad_start, ri_flat, h2, slots)
    ys = _experts(l, n_blocks, blk_e, n_used, xs, w_gate_up, w_down)
    return _combine(n_tiles, pad_start, ri_flat, ys, x, rw, mod_l), ys


def _rope_tables():
    pos = jnp.arange(SEQ, dtype=_F32)
    row = jnp.floor(pos / GRID_W)
    col = pos - row * GRID_W
    n_freq = QK_ROPE // 4
    inv = ROPE_THETA ** (-jnp.arange(n_freq, dtype=_F32) / n_freq)
    ang = jnp.concatenate([row[:, None] * inv, col[:, None] * inv], axis=-1)
    cos, sin = jnp.cos(ang), jnp.sin(ang)
    z32 = jnp.zeros((SEQ, 32), _F32)
    z64 = jnp.zeros((SEQ, 64), _F32)
    c = jnp.concatenate([cos, cos, jnp.ones((SEQ, 64), _F32)], axis=-1)
    s1 = jnp.concatenate([-sin, z32, z64], axis=-1)
    s2 = jnp.concatenate([z32, sin, z64], axis=-1)
    ctx_c = jnp.ones((CTX_LEN, 128), _F32)
    ctx_s = jnp.zeros((CTX_LEN, 128), _F32)
    return (jnp.concatenate([c, ctx_c]), jnp.concatenate([s1, ctx_s]), jnp.concatenate([s2, ctx_s]))


def _pad_head_gain(g):
    return jnp.pad(g, ((0, 0), (0, QK_PAD - QK_DIM)))[:, None, :]


def kernel(x, c, ctx, c_ctx, w_mod, b_mod, norm1_g, norm2_g, w_in, q_a_norm_g, kv_a_norm_g, w_uq, w_ukv, q_norm_g, k_norm_g, conv_w, conv_b, lru_wa, lru_ba, lru_wx, lru_bx, lru_lambda, attn_out_norm_g, rnn_out_norm_g, w_out, router_group_w, router_group_b, router_expert_w, router_expert_b, w_gate_up, w_down):
    L = DEPTH
    zc = jnp.zeros((L, D_MODEL, 64), _F32)
    w_in_p = jnp.concatenate([w_in[:, :, 0:832], zc, w_in[:, :, 832:2880]], axis=-1).astype(_BF16)
    w_uq_p = jnp.pad(w_uq.reshape(L, Q_RANK, N_HEADS, QK_DIM), ((0, 0), (0, 0), (0, 0), (0, QK_PAD - QK_DIM)))
    w_uq_p = w_uq_p.reshape(L, Q_RANK, N_HEADS * QK_PAD).astype(_BF16)
    w_ukv_b = w_ukv.astype(_BF16)
    w_out_b = w_out.astype(_BF16)
    qg_p = _pad_head_gain(q_norm_g)
    kgn = k_norm_g[:, None, 0:QK_NOPE]
    kgr = jnp.pad(k_norm_g[:, QK_NOPE:QK_DIM], ((0, 0), (0, 64)))[:, None, :]
    wr = jnp.pad(jnp.concatenate([router_group_w, router_expert_w], axis=-1),
                 ((0, 0), (0, 0), (0, ROUTE_LANES - N_GROUPS - N_EXPERTS)))
    wr_hi = wr.astype(_BF16)
    wr = jnp.concatenate([wr_hi, (wr - wr_hi.astype(_F32)).astype(_BF16)], axis=-1)
    br = jnp.pad(jnp.concatenate([router_group_b, router_expert_b], axis=-1),
                 ((0, 0), (0, ROUTE_LANES - N_GROUPS - N_EXPERTS)))[:, None, :]
    rope_c, rope_s1, rope_s2 = _rope_tables()

    cond_t = jnp.stack([c[0], c_ctx], axis=-1)
    mod = _adaln(cond_t, w_mod, b_mod).reshape(L, 2, 6, D_MODEL)

    x_lat, x_ctx, ctx_blk = x[0], ctx[0], 0
    slots = jnp.zeros((MOE_BLOCKS * MOE_BLK, D_MODEL), _F32)
    for l in range(L):
        last = l == L - 1
        n_tiles = NT_LAT if last else NT_ALL
        pq, pkv, rx, rg = _proj_in(x_lat, x_ctx, ctx_blk, mod[l], norm1_g[l][None, :], w_in_p[l])
        q, k, v = _mla_prep(pq, pkv, q_a_norm_g[l][None, :], kv_a_norm_g[l][None, :], w_uq_p[l], w_ukv_b[l],
                            qg_p[l], kgn[l], kgr[l], rope_c, rope_s1, rope_s2)
        attn_lat = _attention(q, k, v, q_tile0=0, n_q=SEQ, tq=TQ, key_tile0=0, n_keys=T_ALL, tk=TM * 11)
        if last:
            attn_ctx = attn_lat[0:TM]
        else:
            attn_ctx = _attention(q, k, v, q_tile0=CTX_TILE, n_q=CTX_LEN, tq=CTX_LEN, key_tile0=CTX_TILE,
                                  n_keys=CTX_LEN, tk=CTX_LEN)
        h_f = _rnn(l, 0, rx, conv_w, conv_b, lru_wa, lru_ba, lru_wx, lru_bx, lru_lambda)
        y_rnn = _rnn(l, 1, rx, conv_w, conv_b, lru_wa, lru_ba, lru_wx, lru_bx, lru_lambda,
                     extra=(h_f, rg, rnn_out_norm_g[l][None, :]))
        x_mid, h2, ri, rw, counts = _out_proj(n_tiles, attn_lat, attn_ctx, y_rnn, attn_out_norm_g[l][None, :],
                                              w_out_b[l], x_lat, x_ctx, ctx_blk, mod[l], norm2_g[l][None, :],
                                              wr[l], br[l])
        x_lat, slots = _moe(l, n_tiles, h2, ri, rw, counts, x_mid, mod[l], w_gate_up, w_down, slots)
        x_ctx, ctx_blk = x_lat, CTX_TILE
    return x_lat.reshape(1, SEQ, D_MODEL)
```
